```python
import jax, jax.numpy as jnp
from jax import lax
import numpy as np

D_MODEL = 2048
BATCH = 1
SEQ = 8192
DEPTH = 2
DEC_BATCH = 128
DEC_SEQ = 1
PAST_LEN = 2048
PAGE_SIZE = 128

HEAD_DIM = 128
N_HEADS = D_MODEL // HEAD_DIM
Q_BLOCK = 128
POOL_WINDOWS = (2, 4, 8, 16)
N_GROUPS = len(POOL_WINDOWS)
GROUP_DIM = D_MODEL // N_GROUPS
POOL_BUF = max(POOL_WINDOWS) - 1
D_FF = 4 * D_MODEL
N_ATTN_LAYERS = (DEPTH + 1) // 2
N_POOL_LAYERS = DEPTH // 2
ALPHA = (2.0 * DEPTH) ** 0.25
BETA = (8.0 * DEPTH) ** -0.25
LN_EPS = 1e-5
ATTN_SCALE = HEAD_DIM ** -0.5
REACH_MIN = 16.0
REACH_MAX = 8192.0

kernel_name = "stickbreak_pool_hybrid_step"


def layer_norm(x, g, b):
    xf = x.astype(jnp.float32)
    mu = jnp.mean(xf, axis=-1, keepdims=True)
    xc = xf - mu
    var = jnp.mean(xc * xc, axis=-1, keepdims=True)
    return (xc * lax.rsqrt(var + LN_EPS) * g.astype(jnp.float32) + b.astype(jnp.float32)).astype(x.dtype)


def squared_relu_mlp(x, w_up, w_down):
    h = jax.nn.relu(x @ w_up)
    return (h * h) @ w_down


def qkv_split(x, w_qkv):
    b, t, _ = x.shape
    qkv = (x @ w_qkv).reshape(b, t, 3, N_HEADS, HEAD_DIM)
    return qkv[:, :, 0], qkv[:, :, 1], qkv[:, :, 2]


def stick_breaking_attend(q, k, v, bias, q_pos, k_pos):
    z = jnp.einsum('bqhd,bkhd->bhqk', q, k).astype(jnp.float32) * ATTN_SCALE
    z = z + bias.astype(jnp.float32)[None, :, None, None]
    mask = k_pos[None, :] < q_pos[:, None]
    log_stay = jnp.where(mask, jax.nn.log_sigmoid(-z), 0.0)
    rest = lax.cumsum(log_stay, axis=3, reverse=True) - log_stay
    a = jnp.where(mask, jnp.exp(jax.nn.log_sigmoid(z) + rest), 0.0)
    return jnp.einsum('bhqk,bkhd->bqhd', a.astype(v.dtype), v)


def stick_breaking_prompt(q, k, v, bias):
    b, s, h, d = q.shape
    nb = s // Q_BLOCK
    qb = q.reshape(b, nb, Q_BLOCK, h, d).transpose(1, 0, 2, 3, 4)
    starts = jnp.arange(nb, dtype=jnp.int32) * Q_BLOCK
    k_pos = jnp.arange(s, dtype=jnp.int32)

    def one_block(args):
        q_blk, start = args
        return stick_breaking_attend(q_blk, k, v, bias, start + jnp.arange(Q_BLOCK, dtype=jnp.int32), k_pos)

    o = lax.map(one_block, (qb, starts))
    return o.transpose(1, 0, 2, 3, 4).reshape(b, s, h * d)


def pool_mix(u_new, buf, pos_start, w_group, scale, w_out):
    b, t, d = u_new.shape
    u_ext = jnp.concatenate([buf.astype(u_new.dtype), u_new], axis=1)
    c = jnp.cumsum(u_ext.astype(jnp.float32), axis=1)
    c = jnp.concatenate([jnp.zeros((b, 1, d), jnp.float32), c], axis=1)
    pos1 = pos_start + jnp.arange(t, dtype=jnp.int32) + 1
    parts = []
    for g, w in enumerate(POOL_WINDOWS):
        lo, hi = g * GROUP_DIM, (g + 1) * GROUP_DIM
        win_sum = c[:, POOL_BUF + 1:, lo:hi] - c[:, POOL_BUF + 1 - w:POOL_BUF + 1 - w + t, lo:hi]
        cnt = jnp.minimum(pos1, w).astype(jnp.float32)[None, :, None]
        parts.append(win_sum / cnt)
    mean = jnp.concatenate(parts, axis=-1)
    pooled = (mean - u_new.astype(jnp.float32)).astype(u_new.dtype)
    h = jnp.einsum('btgc,gce->btge', pooled.reshape(b, t, N_GROUPS, GROUP_DIM), w_group).reshape(b, t, d)
    return (h * scale) @ w_out, u_ext[:, -POOL_BUF:]


def setup_inputs(seed: int = 0) -> dict:
    key = jax.random.key(seed)
    ks = jax.random.split(key, 20)
    n_pages = PAST_LEN // PAGE_SIZE
    n_used = DEC_BATCH * n_pages
    n_phys = (5 * n_used + 3) // 4
    f32 = jnp.float32
    x_prompt = jax.random.normal(ks[0], (BATCH, SEQ, D_MODEL), f32)
    x_sample = jax.random.normal(ks[1], (DEC_BATCH, DEC_SEQ, D_MODEL), f32)
    cache_k = jax.random.normal(ks[2], (N_ATTN_LAYERS, n_phys, PAGE_SIZE, N_HEADS, HEAD_DIM), f32)
    cache_v = jax.random.normal(ks[3], (N_ATTN_LAYERS, n_phys, PAGE_SIZE, N_HEADS, HEAD_DIM), f32)
    state_pool = jax.random.normal(ks[4], (N_POOL_LAYERS, DEC_BATCH, POOL_BUF, D_MODEL), f32)
    page_table = jax.random.permutation(ks[5], n_phys)[:n_used].reshape(DEC_BATCH, n_pages).astype(jnp.int32)
    ln_g = 1.0 + 0.02 * jax.random.normal(ks[6], (DEPTH, 2, D_MODEL), f32)
    ln_b = 0.02 * jax.random.normal(ks[7], (DEPTH, 2, D_MODEL), f32)
    w_qkv = jax.random.normal(ks[8], (N_ATTN_LAYERS, D_MODEL, 3 * D_MODEL), f32) * D_MODEL ** -0.5
    log_reach = jnp.linspace(jnp.log(REACH_MIN), jnp.log(REACH_MAX), N_HEADS, dtype=f32)
    attn_bias = -log_reach[None, :] + 0.02 * jax.random.normal(ks[16], (N_ATTN_LAYERS, N_HEADS), f32)
    w_attn_out = jax.random.normal(ks[9], (N_ATTN_LAYERS, D_MODEL, D_MODEL), f32) * (D_MODEL ** -0.5 * BETA)
    w_pool_in = jax.random.normal(ks[10], (N_POOL_LAYERS, D_MODEL, D_MODEL), f32) * D_MODEL ** -0.5
    w_pool_group = jax.random.normal(ks[11], (N_POOL_LAYERS, N_GROUPS, GROUP_DIM, GROUP_DIM), f32) * GROUP_DIM ** -0.5
    pool_scale = 1.0 + 0.02 * jax.random.normal(ks[12], (N_POOL_LAYERS, D_MODEL), f32)
    w_pool_out = jax.random.normal(ks[13], (N_POOL_LAYERS, D_MODEL, D_MODEL), f32) * (D_MODEL ** -0.5 * BETA)
    w_mlp_up = jax.random.normal(ks[14], (DEPTH, D_MODEL, D_FF), f32) * D_MODEL ** -0.5
    w_mlp_down = jax.random.normal(ks[15], (DEPTH, D_FF, D_MODEL), f32) * (D_FF ** -0.5 * BETA)
    return {"x_prompt": x_prompt, "x_sample": x_sample, "cache_k": cache_k, "cache_v": cache_v,
            "state_pool": state_pool, "page_table": page_table, "ln_g": ln_g, "ln_b": ln_b,
            "w_qkv": w_qkv, "attn_bias": attn_bias, "w_attn_out": w_attn_out, "w_pool_in": w_pool_in,
            "w_pool_group": w_pool_group, "pool_scale": pool_scale, "w_pool_out": w_pool_out,
            "w_mlp_up": w_mlp_up, "w_mlp_down": w_mlp_down}


def reference(x_prompt, x_sample, cache_k, cache_v, state_pool, page_table, ln_g, ln_b, w_qkv, attn_bias,
              w_attn_out, w_pool_in, w_pool_group, pool_scale, w_pool_out, w_mlp_up, w_mlp_down):
    yp, ys = x_prompt, x_sample
    dec_b, dec_t, _ = x_sample.shape
    past_len = page_table.shape[1] * cache_k.shape[2]
    kp_list, vp_list, ks_list, vs_list, pp_list, ps_list = [], [], [], [], [], []
    for i in range(DEPTH):
        if i % 2 == 0:
            a = i // 2
            qp, kp, vp = qkv_split(yp, w_qkv[a])
            mp = stick_breaking_prompt(qp, kp, vp, attn_bias[a]) @ w_attn_out[a]
            qs, ks_new, vs_new = qkv_split(ys, w_qkv[a])
            k_past = cache_k[a][page_table].reshape(dec_b, past_len, N_HEADS, HEAD_DIM).astype(ks_new.dtype)
            v_past = cache_v[a][page_table].reshape(dec_b, past_len, N_HEADS, HEAD_DIM).astype(vs_new.dtype)
            k_all = jnp.concatenate([k_past, ks_new], axis=1)
            v_all = jnp.concatenate([v_past, vs_new], axis=1)
            q_pos = past_len + jnp.arange(dec_t, dtype=jnp.int32)
            k_pos = jnp.arange(past_len + dec_t, dtype=jnp.int32)
            ms = stick_breaking_attend(qs, k_all, v_all, attn_bias[a], q_pos, k_pos).reshape(dec_b, dec_t, D_MODEL) @ w_attn_out[a]
            kp_list.append(kp); vp_list.append(vp); ks_list.append(ks_new); vs_list.append(vs_new)
        else:
            p = i // 2
            up = yp @ w_pool_in[p]
            zero_buf = jnp.zeros((yp.shape[0], POOL_BUF, D_MODEL), up.dtype)
            mp, buf_p = pool_mix(up, zero_buf, 0, w_pool_group[p], pool_scale[p], w_pool_out[p])
            us = ys @ w_pool_in[p]
            ms, buf_s = pool_mix(us, state_pool[p], past_len, w_pool_group[p], pool_scale[p], w_pool_out[p])
            pp_list.append(buf_p); ps_list.append(buf_s)
        yp = layer_norm(ALPHA * yp + mp, ln_g[i, 0], ln_b[i, 0])
        ys = layer_norm(ALPHA * ys + ms, ln_g[i, 0], ln_b[i, 0])
        yp = layer_norm(ALPHA * yp + squared_relu_mlp(yp, w_mlp_up[i], w_mlp_down[i]), ln_g[i, 1], ln_b[i, 1])
        ys = layer_norm(ALPHA * ys + squared_relu_mlp(ys, w_mlp_up[i], w_mlp_down[i]), ln_g[i, 1], ln_b[i, 1])
    k_prompt = jnp.stack(kp_list)
    v_prompt = jnp.stack(vp_list)
    pool_prompt = jnp.stack(pp_list)
    k_sample = jnp.stack(ks_list)
    v_sample = jnp.stack(vs_list)
    pool_sample = jnp.stack(ps_list)
    return (yp, ys, k_prompt, v_prompt, pool_prompt, k_sample, v_sample, pool_sample)
```

```python
import functools
import math

import jax
import jax.numpy as jnp
from jax import lax
from jax.experimental import pallas as pl
from jax.experimental.pallas import tpu as pltpu

F32 = jnp.float32
BF16 = jnp.bfloat16

HEAD_DIM = 128
POOL_WINDOWS = (2, 4, 8, 16)
LN_EPS = 1e-5

LANES = 128
VMEM_LIMIT_BYTES = 56 * 1024 * 1024

ATTN_TQ = 256
ATTN_SUB = 128
ATTN_TK = 2 * ATTN_SUB
ATTN_HEADS_PER_STEP = 2
DEC_PAGES_PER_STEP = 4


def _params(*semantics):
    return pltpu.CompilerParams(dimension_semantics=semantics, vmem_limit_bytes=VMEM_LIMIT_BYTES)


def _layer_norm(xf, g, b):
    mu = jnp.mean(xf, axis=-1, keepdims=True)
    xc = xf - mu
    var = jnp.mean(xc * xc, axis=-1, keepdims=True)
    return xc * lax.rsqrt(var + LN_EPS) * g + b


def _matmul_kernel(a_ref, b_ref, *o_refs, scale):
    acc = jnp.dot(a_ref[...].astype(BF16), b_ref[...], preferred_element_type=F32)
    for o_ref in o_refs:
        if o_ref.dtype == BF16 and scale is not None:
            o_ref[...] = (acc * scale).astype(BF16)
        else:
            o_ref[...] = acc.astype(o_ref.dtype)


def _matmul(a, b, out_dtypes, *, scale=None, tm=1024, tn=1024):
    m, k = a.shape
    n = b.shape[1]
    tm, tn = min(tm, m), min(tn, n)
    outs = pl.pallas_call(
        functools.partial(_matmul_kernel, scale=scale),
        grid=(m // tm, n // tn),
        in_specs=[pl.BlockSpec((tm, k), lambda i, j: (i, 0)),
                  pl.BlockSpec((k, tn), lambda i, j: (0, j))],
        out_specs=[pl.BlockSpec((tm, tn), lambda i, j: (i, j)) for _ in out_dtypes],
        out_shape=[jax.ShapeDtypeStruct((m, n), dt) for dt in out_dtypes],
        compiler_params=_params("arbitrary", "arbitrary"),
        name="matmul",
    )(a, b)
    return outs


def _proj_ln_kernel(a_ref, w_ref, x_ref, g_ref, b_ref, o_ref, *, alpha):
    m = jnp.dot(a_ref[...].astype(BF16), w_ref[...], preferred_element_type=F32)
    o_ref[...] = _layer_norm(alpha * x_ref[...] + m, g_ref[...], b_ref[...])


def _proj_ln(a, w, x, g, b, *, alpha, tm=256):
    m, k = a.shape
    d = w.shape[1]
    tm = min(tm, m)
    return pl.pallas_call(
        functools.partial(_proj_ln_kernel, alpha=alpha),
        grid=(m // tm,),
        in_specs=[pl.BlockSpec((tm, k), lambda i: (i, 0)),
                  pl.BlockSpec((k, d), lambda i: (0, 0)),
                  pl.BlockSpec((tm, d), lambda i: (i, 0)),
                  pl.BlockSpec((1, d), lambda i: (0, 0)),
                  pl.BlockSpec((1, d), lambda i: (0, 0))],
        out_specs=pl.BlockSpec((tm, d), lambda i: (i, 0)),
        out_shape=jax.ShapeDtypeStruct((m, d), F32),
        compiler_params=_params("arbitrary"),
        name="proj_ln",
    )(a, w, x, g, b)


def _mlp_kernel(x_ref, wu_ref, wd_ref, g_ref, b_ref, o_ref, acc_ref, xb_ref, *, alpha):
    kf = pl.program_id(1)

    @pl.when(kf == 0)
    def _():
        xb_ref[...] = x_ref[...].astype(BF16)
        acc_ref[...] = jnp.zeros_like(acc_ref)

    h = jnp.dot(xb_ref[...], wu_ref[...], preferred_element_type=F32)
    h = jnp.maximum(h, 0.0)
    acc_ref[...] += jnp.dot((h * h).astype(BF16), wd_ref[...], preferred_element_type=F32)

    @pl.when(kf == pl.num_programs(1) - 1)
    def _():
        o_ref[...] = _layer_norm(alpha * x_ref[...] + acc_ref[...], g_ref[...], b_ref[...])


def _mlp(x, wu, wd, g, b, *, alpha, tm=512, tf=1024):
    m, d = x.shape
    f = wu.shape[1]
    tm = min(tm, m)
    return pl.pallas_call(
        functools.partial(_mlp_kernel, alpha=alpha),
        grid=(m // tm, f // tf),
        in_specs=[pl.BlockSpec((tm, d), lambda i, kf: (i, 0)),
                  pl.BlockSpec((d, tf), lambda i, kf: (0, kf)),
                  pl.BlockSpec((tf, d), lambda i, kf: (kf, 0)),
                  pl.BlockSpec((1, d), lambda i, kf: (0, 0)),
                  pl.BlockSpec((1, d), lambda i, kf: (0, 0))],
        out_specs=pl.BlockSpec((tm, d), lambda i, kf: (i, 0)),
        out_shape=jax.ShapeDtypeStruct((m, d), F32),
        scratch_shapes=[pltpu.VMEM((tm, d), F32), pltpu.VMEM((tm, d), BF16)],
        compiler_params=_params("arbitrary", "arbitrary"),
        name="mlp",
    )(x, wu, wd, g, b)


def _log_stay_and_beta(w):
    ls = jnp.minimum(w, 0.0) - jnp.log(1.0 + jnp.exp(-jnp.abs(w)))
    return ls, ls - w


def _split_bf16(x):
    hi = x.astype(BF16)
    lo = (x - hi.astype(F32)).astype(BF16)
    return hi, lo


def _attn_sub_block(k_sub, q_t, nb, u2, carry, mask):
    w = jnp.dot(k_sub, q_t, preferred_element_type=F32) + nb
    ls, lb = _log_stay_and_beta(w)
    if mask is not None:
        ls = jnp.where(mask, ls, 0.0)
    hi, lo = _split_bf16(ls)
    rest = jnp.dot(u2, jnp.concatenate([hi, lo], axis=0), preferred_element_type=F32)
    a = jnp.exp(lb + rest + carry)
    if mask is not None:
        a = jnp.where(mask, a, 0.0)
    return a.astype(BF16), carry + jnp.sum(ls, axis=0, keepdims=True)


def _attn_kernel(nb_ref, qt_ref, k_ref, vt_ref, u_ref, o_ref):
    hp = pl.program_id(0)
    i = pl.program_id(1)
    u2 = u_ref[...]
    row = lax.broadcasted_iota(jnp.int32, (ATTN_SUB, ATTN_TQ), 0)
    col = lax.broadcasted_iota(jnp.int32, (ATTN_SUB, ATTN_TQ), 1)
    diag_masks = (row < col, row + ATTN_SUB < col)

    def block(j, state, masks):
        start = pl.multiple_of(j * ATTN_TK, ATTN_TK)
        new_state = []
        for hh in range(ATTN_HEADS_PER_STEP):
            acc, carry = state[hh]
            nb = nb_ref[hp * ATTN_HEADS_PER_STEP + hh]
            q_t = qt_ref[hh]
            lanes = slice(hh * HEAD_DIM, (hh + 1) * HEAD_DIM)
            a_hi, carry = _attn_sub_block(k_ref[pl.ds(start + ATTN_SUB, ATTN_SUB), lanes], q_t, nb, u2,
                                          carry, None if masks is None else masks[1])
            a_lo, carry = _attn_sub_block(k_ref[pl.ds(start, ATTN_SUB), lanes], q_t, nb, u2,
                                          carry, None if masks is None else masks[0])
            a = jnp.concatenate([a_lo, a_hi], axis=0)
            acc = acc + jnp.dot(vt_ref[hh, j], a, preferred_element_type=F32)
            new_state.append((acc, carry))
        return tuple(new_state)

    zero = (jnp.zeros((HEAD_DIM, ATTN_TQ), F32), jnp.zeros((1, ATTN_TQ), F32))
    state = block(i, tuple(zero for _ in range(ATTN_HEADS_PER_STEP)), diag_masks)
    state = lax.fori_loop(0, i, lambda t, s: block(i - 1 - t, s, None), state)
    for hh in range(ATTN_HEADS_PER_STEP):
        o_ref[:, hh * HEAD_DIM:(hh + 1) * HEAD_DIM] = state[hh][0].T.astype(o_ref.dtype)


def _attention_prompt(q_t, k_b, v_t, nbias, u2):
    n_heads, hd, s = q_t.shape
    hps = ATTN_HEADS_PER_STEP
    return pl.pallas_call(
        _attn_kernel,
        grid=(n_heads // hps, s // ATTN_TQ),
        in_specs=[pl.BlockSpec(memory_space=pltpu.SMEM),
                  pl.BlockSpec((hps, hd, ATTN_TQ), lambda h, i: (h, 0, i)),
                  pl.BlockSpec((s, hps * hd), lambda h, i: (0, h)),
                  pl.BlockSpec((hps, s // ATTN_TK, hd, ATTN_TK), lambda h, i: (h, 0, 0, 0)),
                  pl.BlockSpec((ATTN_SUB, 2 * ATTN_SUB), lambda h, i: (0, 0))],
        out_specs=pl.BlockSpec((ATTN_TQ, hps * hd), lambda h, i: (i, h)),
        out_shape=jax.ShapeDtypeStruct((s, n_heads * hd), BF16),
        compiler_params=_params("arbitrary", "arbitrary"),
        name="attn_prompt",
    )(nbias, q_t, k_b, v_t, u2)


def _decode_kernel(pt_ref, nb_ref, q_ref, *refs, n_heads, scale):
    npp = DEC_PAGES_PER_STEP
    k_refs, v_refs = refs[:npp], refs[npp:2 * npp]
    ul_ref, o_ref, qbd_ref, acc_ref, carry_ref = refs[2 * npp:]
    p = pl.program_id(1)
    d = q_ref.shape[-1]
    head_of_lane = lax.broadcasted_iota(jnp.int32, (n_heads, d), 1) // HEAD_DIM
    own_head = head_of_lane == lax.broadcasted_iota(jnp.int32, (n_heads, d), 0)

    @pl.when(p == 0)
    def _():
        qbd_ref[...] = jnp.where(own_head, q_ref[...] * (-scale), 0.0).astype(BF16)
        acc_ref[...] = jnp.zeros_like(acc_ref)
        carry_ref[...] = jnp.zeros_like(carry_ref)

    qbd = qbd_ref[...]
    nb = nb_ref[...]
    ul = ul_ref[...]
    page = ul.shape[0] // 2
    acc = acc_ref[...]
    carry = carry_ref[...]
    for c in range(npp):
        kp = k_refs[c][...].astype(BF16)
        w = lax.dot_general(qbd, kp, (((1,), (1,)), ((), ())), preferred_element_type=F32) + nb
        ls, lb = _log_stay_and_beta(w)
        hi, lo = _split_bf16(ls)
        scan = jnp.dot(jnp.concatenate([hi, lo], axis=1), ul, preferred_element_type=F32)
        a = jnp.exp(lb + scan[:, :page] + carry)
        carry = carry + scan[:, page:]
        acc = acc + jnp.dot(a.astype(BF16), v_refs[c][...].astype(BF16), preferred_element_type=F32)
    acc_ref[...] = acc
    carry_ref[...] = carry

    @pl.when(p == pl.num_programs(1) - 1)
    def _():
        o_ref[...] = jnp.sum(jnp.where(own_head, acc, 0.0), axis=0, keepdims=True).astype(o_ref.dtype)


def _attention_decode(q, cache_k, cache_v, page_table, nbias, *, scale):
    bsz, _, d = q.shape
    n_heads = d // HEAD_DIM
    page = cache_k.shape[1]
    n_pages = page_table.shape[1]
    npp = DEC_PAGES_PER_STEP
    tri = (jnp.arange(page)[:, None] > jnp.arange(page)[None, :]).astype(BF16)
    ul = jnp.tile(jnp.concatenate([tri, jnp.ones((page, page), BF16)], axis=1), (2, 1))

    def page_spec(c):
        return pl.BlockSpec((None, page, d), lambda b, p, pt: (pt[b, n_pages - 1 - (p * npp + c)], 0, 0))

    grid_spec = pltpu.PrefetchScalarGridSpec(
        num_scalar_prefetch=1,
        grid=(bsz, n_pages // npp),
        in_specs=[pl.BlockSpec((n_heads, 1), lambda b, p, pt: (0, 0)),
                  pl.BlockSpec((None, 1, d), lambda b, p, pt: (b, 0, 0))]
                 + [page_spec(c) for c in range(npp)] + [page_spec(c) for c in range(npp)]
                 + [pl.BlockSpec((2 * page, 2 * page), lambda b, p, pt: (0, 0))],
        out_specs=pl.BlockSpec((None, 1, d), lambda b, p, pt: (b, 0, 0)),
        scratch_shapes=[pltpu.VMEM((n_heads, d), BF16), pltpu.VMEM((n_heads, d), F32),
                        pltpu.VMEM((n_heads, page), F32)],
    )
    return pl.pallas_call(
        functools.partial(_decode_kernel, n_heads=n_heads, scale=scale),
        grid_spec=grid_spec,
        out_shape=jax.ShapeDtypeStruct((bsz, 1, d), BF16),
        compiler_params=_params("arbitrary", "arbitrary"),
        name="attn_decode",
    )(page_table, nbias.reshape(n_heads, 1), q, *([cache_k] * npp), *([cache_v] * npp), ul)


def _group_map(pooled, wg_ref, sc_ref, o_ref):
    gd = wg_ref.shape[1]
    for g in range(len(POOL_WINDOWS)):
        lanes = slice(g * gd, (g + 1) * gd)
        h = jnp.dot(pooled[g].astype(BF16), wg_ref[g], preferred_element_type=F32)
        o_ref[:, lanes] = (h * sc_ref[:, lanes]).astype(o_ref.dtype)


def _pool_prompt_kernel(u_ref, prev_ref, wg_ref, sc_ref, o_ref, *, halo):
    i = pl.program_id(0)
    tm = u_ref.shape[0]
    gd = wg_ref.shape[1]
    prev = jnp.where(i > 0, prev_ref[...], 0.0)
    pos1 = i * tm + lax.broadcasted_iota(jnp.int32, (tm, 1), 0) + 1
    pooled = []
    for g, win in enumerate(POOL_WINDOWS):
        lanes = slice(g * gd, (g + 1) * gd)
        u = u_ref[:, lanes]
        ext = jnp.concatenate([prev[:, lanes], u], axis=0)
        span = 1
        while span < win:
            ext = ext + pltpu.roll(ext, span, 0)
            span *= 2
        cnt = jnp.minimum(pos1, win).astype(F32)
        pooled.append(ext[halo:] / cnt - u)
    _group_map(pooled, wg_ref, sc_ref, o_ref)


def _pool_prompt(u, wg, sc, *, tm=256):
    s, d = u.shape
    halo = 16
    gd = wg.shape[1]
    return pl.pallas_call(
        functools.partial(_pool_prompt_kernel, halo=halo),
        grid=(s // tm,),
        in_specs=[pl.BlockSpec((tm, d), lambda i: (i, 0)),
                  pl.BlockSpec((halo, d), lambda i: (jnp.maximum(i * (tm // halo) - 1, 0), 0)),
                  pl.BlockSpec((len(POOL_WINDOWS), gd, gd), lambda i: (0, 0, 0)),
                  pl.BlockSpec((1, d), lambda i: (0, 0))],
        out_specs=pl.BlockSpec((tm, d), lambda i: (i, 0)),
        out_shape=jax.ShapeDtypeStruct((s, d), BF16),
        compiler_params=_params("arbitrary"),
        name="pool_prompt",
    )(u, u, wg, sc)


def _pool_sample_kernel(u_ref, buf_ref, wg_ref, sc_ref, o_ref, *, pos1):
    gd = wg_ref.shape[1]
    d = u_ref.shape[1]
    nbuf = buf_ref.shape[1] // d
    pooled = []
    for g, win in enumerate(POOL_WINDOWS):
        lanes = slice(g * gd, (g + 1) * gd)
        u = u_ref[:, lanes]
        total = u
        for r in range(nbuf - (win - 1), nbuf):
            total = total + buf_ref[:, r * d + g * gd:r * d + (g + 1) * gd]
        pooled.append(total / float(min(pos1, win)) - u)
    _group_map(pooled, wg_ref, sc_ref, o_ref)


def _pool_sample(u, buf, wg, sc, *, pos1, tb=32):
    bsz, d = u.shape
    nbuf = buf.shape[1]
    gd = wg.shape[1]
    buf = buf.reshape(bsz, nbuf * d)
    return pl.pallas_call(
        functools.partial(_pool_sample_kernel, pos1=pos1),
        grid=(bsz // tb,),
        in_specs=[pl.BlockSpec((tb, d), lambda i: (i, 0)),
                  pl.BlockSpec((tb, nbuf * d), lambda i: (i, 0)),
                  pl.BlockSpec((len(POOL_WINDOWS), gd, gd), lambda i: (0, 0, 0)),
                  pl.BlockSpec((1, d), lambda i: (0, 0))],
        out_specs=pl.BlockSpec((tb, d), lambda i: (i, 0)),
        out_shape=jax.ShapeDtypeStruct((bsz, d), BF16),
        compiler_params=_params("arbitrary"),
        name="pool_sample",
    )(u, buf, wg, sc)


def kernel(x_prompt, x_sample, cache_k, cache_v, state_pool, page_table, ln_g, ln_b, w_qkv, attn_bias,
           w_attn_out, w_pool_in, w_pool_group, pool_scale, w_pool_out, w_mlp_up, w_mlp_down):
    batch, seq, d = x_prompt.shape
    dec_b, dec_t, _ = x_sample.shape
    depth = ln_g.shape[0]
    n_heads = d // HEAD_DIM
    page = cache_k.shape[2]
    past_len = page_table.shape[1] * page
    pool_buf = state_pool.shape[2]
    assert batch == 1 and dec_t == 1 and pool_buf == max(POOL_WINDOWS) - 1
    alpha = (2.0 * depth) ** 0.25
    attn_scale = HEAD_DIM ** -0.5

    yp = x_prompt.reshape(seq, d)
    ys = x_sample.reshape(dec_b, d)
    tri = (jnp.arange(ATTN_SUB)[None, :] > jnp.arange(ATTN_SUB)[:, None]).astype(BF16)
    u2 = jnp.concatenate([tri, tri], axis=1)

    kp_l, vp_l, ks_l, vs_l, pp_l, ps_l = [], [], [], [], [], []
    for layer in range(depth):
        g0, b0 = ln_g[layer, 0].reshape(1, d), ln_b[layer, 0].reshape(1, d)
        g1, b1 = ln_g[layer, 1].reshape(1, d), ln_b[layer, 1].reshape(1, d)
        if layer % 2 == 0:
            a = layer // 2
            wq = w_qkv[a, :, :d].astype(BF16)
            wk = w_qkv[a, :, d:2 * d].astype(BF16)
            wv = w_qkv[a, :, 2 * d:].astype(BF16)
            wo = w_attn_out[a].astype(BF16)
            nbias = -attn_bias[a]
            (qb,) = _matmul(yp, wq, (BF16,), scale=-attn_scale)
            kf, kb = _matmul(yp, wk, (F32, BF16))
            vf, vb = _matmul(yp, wv, (F32, BF16))
            q_t = qb.reshape(seq, n_heads, HEAD_DIM).transpose(1, 2, 0)
            v_t = vb.reshape(seq // ATTN_TK, ATTN_TK, n_heads, HEAD_DIM).transpose(2, 0, 3, 1)
            mp = _attention_prompt(q_t, kb, v_t, nbias, u2)
            kp_l.append(kf.reshape(batch, seq, n_heads, HEAD_DIM))
            vp_l.append(vf.reshape(batch, seq, n_heads, HEAD_DIM))
            (qs,) = _matmul(ys, wq, (F32,))
            (ksn,) = _matmul(ys, wk, (F32,))
            (vsn,) = _matmul(ys, wv, (F32,))
            ms = _attention_decode(qs.reshape(dec_b, 1, d), cache_k[a].reshape(-1, page, d),
                                   cache_v[a].reshape(-1, page, d), page_table, nbias,
                                   scale=attn_scale).reshape(dec_b, d)
            ks_l.append(ksn.reshape(dec_b, dec_t, n_heads, HEAD_DIM))
            vs_l.append(vsn.reshape(dec_b, dec_t, n_heads, HEAD_DIM))
        else:
            p = layer // 2
            wi = w_pool_in[p].astype(BF16)
            wg = w_pool_group[p].astype(BF16)
            wo = w_pool_out[p].astype(BF16)
            sc = pool_scale[p].reshape(1, d)
            (up,) = _matmul(yp, wi, (F32,))
            mp = _pool_prompt(up, wg, sc)
            pp_l.append(up[seq - pool_buf:].reshape(batch, pool_buf, d))
            (us,) = _matmul(ys, wi, (F32,))
            ms = _pool_sample(us, state_pool[p], wg, sc, pos1=past_len + 1)
            ps_l.append(jnp.concatenate([state_pool[p][:, 1:], us[:, None, :]], axis=1))
        wu = w_mlp_up[layer].astype(BF16)
        wd = w_mlp_down[layer].astype(BF16)
        yp = _proj_ln(mp, wo, yp, g0, b0, alpha=alpha)
        ys = _proj_ln(ms, wo, ys, g0, b0, alpha=alpha)
        yp = _mlp(yp, wu, wd, g1, b1, alpha=alpha)
        ys = _mlp(ys, wu, wd, g1, b1, alpha=alpha)

    return (yp.reshape(batch, seq, d), ys.reshape(dec_b, dec_t, d),
            jnp.stack(kp_l), jnp.stack(vp_l), jnp.stack(pp_l),
            jnp.stack(ks_l), jnp.stack(vs_l), jnp.stack(ps_l))
```

```python
import functools
import math

import jax
import jax.numpy as jnp
from jax import lax
from jax.experimental import pallas as pl
from jax.experimental.pallas import tpu as pltpu

F32 = jnp.float32
BF16 = jnp.bfloat16

HEAD_DIM = 128
POOL_WINDOWS = (2, 4, 8, 16)
LN_EPS = 1e-5

LANES = 128
VMEM_LIMIT_BYTES = 56 * 1024 * 1024

ATTN_TQ = 512
ATTN_SUB = 128
ATTN_TK = ATTN_TQ
ATTN_HEADS_PER_STEP = 2
ATTN_TOTAL_ROWS = 16
DEC_PAGES_PER_STEP = 4

LOG2E = math.log2(math.e)
MASKED_LOG2 = -1e30


def _params(*semantics):
    return pltpu.CompilerParams(dimension_semantics=semantics, vmem_limit_bytes=VMEM_LIMIT_BYTES)


def _layer_norm(xf, g, b):
    mu = jnp.mean(xf, axis=-1, keepdims=True)
    xc = xf - mu
    var = jnp.mean(xc * xc, axis=-1, keepdims=True)
    return xc * lax.rsqrt(var + LN_EPS) * g + b


def _matmul_kernel(a_ref, b_ref, *o_refs, scale):
    acc = jnp.dot(a_ref[...].astype(BF16), b_ref[...], preferred_element_type=F32)
    for o_ref in o_refs:
        if o_ref.dtype == BF16 and scale is not None:
            o_ref[...] = (acc * scale).astype(BF16)
        else:
            o_ref[...] = acc.astype(o_ref.dtype)


def _matmul(a, b, out_dtypes, *, scale=None, tm=1024, tn=1024):
    m, k = a.shape
    n = b.shape[1]
    tm, tn = min(tm, m), min(tn, n)
    outs = pl.pallas_call(
        functools.partial(_matmul_kernel, scale=scale),
        grid=(m // tm, n // tn),
        in_specs=[pl.BlockSpec((tm, k), lambda i, j: (i, 0)),
                  pl.BlockSpec((k, tn), lambda i, j: (0, j))],
        out_specs=[pl.BlockSpec((tm, tn), lambda i, j: (i, j)) for _ in out_dtypes],
        out_shape=[jax.ShapeDtypeStruct((m, n), dt) for dt in out_dtypes],
        compiler_params=_params("arbitrary", "arbitrary"),
        name="matmul",
    )(a, b)
    return outs


def _proj_ln_kernel(a_ref, w_ref, x_ref, g_ref, b_ref, o_ref, *, alpha):
    m = jnp.dot(a_ref[...].astype(BF16), w_ref[...], preferred_element_type=F32)
    o_ref[...] = _layer_norm(alpha * x_ref[...] + m, g_ref[...], b_ref[...])


def _proj_ln(a, w, x, g, b, *, alpha, tm=256):
    m, k = a.shape
    d = w.shape[1]
    tm = min(tm, m)
    return pl.pallas_call(
        functools.partial(_proj_ln_kernel, alpha=alpha),
        grid=(m // tm,),
        in_specs=[pl.BlockSpec((tm, k), lambda i: (i, 0)),
                  pl.BlockSpec((k, d), lambda i: (0, 0)),
                  pl.BlockSpec((tm, d), lambda i: (i, 0)),
                  pl.BlockSpec((1, d), lambda i: (0, 0)),
                  pl.BlockSpec((1, d), lambda i: (0, 0))],
        out_specs=pl.BlockSpec((tm, d), lambda i: (i, 0)),
        out_shape=jax.ShapeDtypeStruct((m, d), F32),
        compiler_params=_params("arbitrary"),
        name="proj_ln",
    )(a, w, x, g, b)


def _mlp_kernel(x_ref, wu_ref, wd_ref, g_ref, b_ref, o_ref, acc_ref, xb_ref, *, alpha):
    kf = pl.program_id(1)

    @pl.when(kf == 0)
    def _():
        xb_ref[...] = x_ref[...].astype(BF16)
        acc_ref[...] = jnp.zeros_like(acc_ref)

    h = jnp.dot(xb_ref[...], wu_ref[...], preferred_element_type=F32)
    h = jnp.maximum(h, 0.0)
    acc_ref[...] += jnp.dot((h * h).astype(BF16), wd_ref[...], preferred_element_type=F32)

    @pl.when(kf == pl.num_programs(1) - 1)
    def _():
        o_ref[...] = _layer_norm(alpha * x_ref[...] + acc_ref[...], g_ref[...], b_ref[...])


def _mlp(x, wu, wd, g, b, *, alpha, tm=512, tf=1024):
    m, d = x.shape
    f = wu.shape[1]
    tm = min(tm, m)
    return pl.pallas_call(
        functools.partial(_mlp_kernel, alpha=alpha),
        grid=(m // tm, f // tf),
        in_specs=[pl.BlockSpec((tm, d), lambda i, kf: (i, 0)),
                  pl.BlockSpec((d, tf), lambda i, kf: (0, kf)),
                  pl.BlockSpec((tf, d), lambda i, kf: (kf, 0)),
                  pl.BlockSpec((1, d), lambda i, kf: (0, 0)),
                  pl.BlockSpec((1, d), lambda i, kf: (0, 0))],
        out_specs=pl.BlockSpec((tm, d), lambda i, kf: (i, 0)),
        out_shape=jax.ShapeDtypeStruct((m, d), F32),
        scratch_shapes=[pltpu.VMEM((tm, d), F32), pltpu.VMEM((tm, d), BF16)],
        compiler_params=_params("arbitrary", "arbitrary"),
        name="mlp",
    )(x, wu, wd, g, b)


def _log2_stay_and_beta(w):
    ls = jnp.minimum(w, 0.0) - jnp.log(1.0 + jnp.exp2(-jnp.abs(w))) * LOG2E
    return ls, ls - w


def _split_bf16(x):
    hi = x.astype(BF16)
    lo = (x - hi.astype(F32)).astype(BF16)
    return hi, lo


def _attn_block(k_ref, vt_ref, k_one, u3, qt_ref, j, state, diagonal):
    start = pl.multiple_of(j * ATTN_TK, ATTN_TK)
    heads = range(ATTN_HEADS_PER_STEP)
    subs = list(reversed(range(ATTN_TK // ATTN_SUB)))
    scores = {}
    for hh in heads:
        q_t = qt_ref[hh]
        for sub in subs:
            k_sub = k_ref[pl.ds(start + sub * ATTN_SUB, ATTN_SUB), hh * HEAD_DIM:(hh + 1) * HEAD_DIM]
            scores[hh, sub] = jnp.dot(jnp.concatenate([k_sub, k_one], axis=1), q_t, preferred_element_type=F32)
    log_beta, split = {}, {}
    for hh in heads:
        for sub in subs:
            ls, lb = _log2_stay_and_beta(scores[hh, sub])
            if diagonal:
                row = lax.broadcasted_iota(jnp.int32, ls.shape, 0)
                col = lax.broadcasted_iota(jnp.int32, ls.shape, 1)
                mask = row + sub * ATTN_SUB < col
                ls = jnp.where(mask, ls, 0.0)
                lb = jnp.where(mask, lb, MASKED_LOG2)
            log_beta[hh, sub] = lb
            split[hh, sub] = jnp.concatenate(_split_bf16(ls), axis=0)
    scans = {key: jnp.dot(u3, hl, preferred_element_type=F32) for key, hl in split.items()}
    new_state = []
    for hh in heads:
        acc, carry = state[hh]
        weights = []
        for sub in subs:
            res = scans[hh, sub]
            weights.append(jnp.exp2(log_beta[hh, sub] + res[:ATTN_SUB] + carry))
            carry = carry + res[ATTN_SUB:ATTN_SUB + 1]
        a = jnp.concatenate(weights[::-1], axis=0).astype(BF16)
        new_state.append((acc + jnp.dot(vt_ref[hh, j], a, preferred_element_type=F32), carry))
    return tuple(new_state)


def _attn_kernel(qt_ref, qbias_ref, k_ref, vt_ref, k_one_ref, u_ref, o_ref):
    i = pl.program_id(1)
    u3 = u_ref[...]
    k_one = k_one_ref[...]
    heads = range(ATTN_HEADS_PER_STEP)
    q_aug = [jnp.concatenate([qt_ref[hh], qbias_ref[hh]], axis=0) for hh in heads]

    def block(j, state, diagonal):
        return _attn_block(k_ref, vt_ref, k_one, u3, q_aug, j, state, diagonal)

    zero = (jnp.zeros((HEAD_DIM, ATTN_TQ), F32), jnp.zeros((1, ATTN_TQ), F32))
    state = block(i, tuple(zero for _ in heads), True)
    state = lax.fori_loop(0, i, lambda t, s: block(i - 1 - t, s, False), state)
    for hh in heads:
        o_ref[:, hh * HEAD_DIM:(hh + 1) * HEAD_DIM] = state[hh][0].T.astype(o_ref.dtype)


def _attention_prompt(q_t, nbias2, k_b, v_t):
    n_heads, hd, s = q_t.shape
    hps = ATTN_HEADS_PER_STEP
    nb_hi = nbias2.astype(BF16)
    nb_lo = (nbias2 - nb_hi.astype(F32)).astype(BF16)
    q_bias = jnp.zeros((n_heads, hd, ATTN_TQ), BF16)
    q_bias = q_bias.at[:, 0, :].set(nb_hi[:, None]).at[:, 1, :].set(nb_lo[:, None])
    k_one = jnp.zeros((ATTN_SUB, hd), BF16).at[:, :2].set(1.0)
    tri = (jnp.arange(ATTN_SUB)[None, :] > jnp.arange(ATTN_SUB)[:, None]).astype(BF16)
    u3 = jnp.concatenate([jnp.concatenate([tri, tri], axis=1),
                          jnp.ones((ATTN_TOTAL_ROWS, 2 * ATTN_SUB), BF16)], axis=0)
    return pl.pallas_call(
        _attn_kernel,
        grid=(n_heads // hps, s // ATTN_TQ),
        in_specs=[pl.BlockSpec((hps, hd, ATTN_TQ), lambda h, i: (h, 0, i)),
                  pl.BlockSpec((hps, hd, ATTN_TQ), lambda h, i: (h, 0, 0)),
                  pl.BlockSpec((s, hps * hd), lambda h, i: (0, h)),
                  pl.BlockSpec((hps, s // ATTN_TK, hd, ATTN_TK), lambda h, i: (h, 0, 0, 0)),
                  pl.BlockSpec(k_one.shape, lambda h, i: (0, 0)),
                  pl.BlockSpec(u3.shape, lambda h, i: (0, 0))],
        out_specs=pl.BlockSpec((ATTN_TQ, hps * hd), lambda h, i: (i, h)),
        out_shape=jax.ShapeDtypeStruct((s, n_heads * hd), BF16),
        compiler_params=_params("arbitrary", "arbitrary"),
        name="attn_prompt",
    )(q_t, q_bias, k_b, v_t, k_one, u3)


def _decode_kernel(pt_ref, nb_ref, q_ref, *refs, qscale):
    npp = DEC_PAGES_PER_STEP
    k_refs, v_refs = refs[:npp], refs[npp:2 * npp]
    scan_ref, o_ref, q2_ref, acc_ref, carry_ref = refs[2 * npp:]
    p = pl.program_id(1)
    n_heads, hd = q_ref.shape
    page = k_refs[0].shape[0] // n_heads
    n_pairs = n_heads // 2
    row = lax.broadcasted_iota(jnp.int32, (n_heads, 2 * hd), 0)
    col = lax.broadcasted_iota(jnp.int32, (n_heads, 2 * hd), 1)
    own_half = (row % 2) == (col // hd)
    row_tok = lax.broadcasted_iota(jnp.int32, (n_heads, 2 * page), 0)

    @pl.when(p == 0)
    def _():
        qs = q_ref[...] * qscale
        qq = jnp.concatenate([qs, qs], axis=1)
        for pr in range(n_pairs):
            q2_ref[pr] = jnp.where((row // 2 == pr) & own_half, qq, 0.0).astype(BF16)
        acc_ref[...] = jnp.zeros_like(acc_ref)
        carry_ref[...] = jnp.zeros_like(carry_ref)

    def head_rows(page_ref, h):
        return page_ref[pl.ds(h, page, stride=n_heads), :]

    def pair_tile(page_refs, older, newer, pr):
        rows = [jnp.concatenate([head_rows(page_refs[pg], 2 * pr), head_rows(page_refs[pg], 2 * pr + 1)], axis=1)
                for pg in (older, newer)]
        return jnp.concatenate(rows, axis=0).astype(BF16)

    nb = nb_ref[...]
    acc = acc_ref[...]
    carry = carry_ref[...]
    for c in range(0, npp, 2):
        z = None
        for pr in range(n_pairs):
            zp = lax.dot_general(q2_ref[pr], pair_tile(k_refs, c + 1, c, pr), (((1,), (1,)), ((), ())),
                                 preferred_element_type=F32)
            z = zp if z is None else z + zp
        ls, lb = _log2_stay_and_beta(z + nb)
        hi, lo = _split_bf16(ls)
        scan = jnp.dot(jnp.concatenate([hi, lo], axis=1), scan_ref[...], preferred_element_type=F32)
        a = jnp.exp2(lb + scan[:, :2 * page] + jnp.concatenate([carry] * (2 * page // LANES), axis=1))
        carry = carry + scan[:, 2 * page:]
        for pr in range(n_pairs):
            a_pr = jnp.where(row_tok // 2 == pr, a, 0.0).astype(BF16)
            acc = acc + jnp.dot(a_pr, pair_tile(v_refs, c + 1, c, pr), preferred_element_type=F32)
    acc_ref[...] = acc
    carry_ref[...] = carry

    @pl.when(p == pl.num_programs(1) - 1)
    def _():
        o_ref[...] = jnp.where(row[:, :hd] % 2 == 0, acc[:, :hd], acc[:, hd:]).astype(o_ref.dtype)


def _attention_decode(q, cache_k, cache_v, page_table, nbias2, *, layer, qscale):
    bsz, n_heads, hd = q.shape
    page = cache_k.shape[2]
    n_pages = page_table.shape[1]
    npp = DEC_PAGES_PER_STEP
    assert npp % 2 == 0 and n_pages % npp == 0 and n_heads % 2 == 0
    tri = (jnp.arange(2 * page)[:, None] > jnp.arange(2 * page)[None, :]).astype(BF16)
    scan_mat = jnp.tile(jnp.concatenate([tri, jnp.ones((2 * page, LANES), BF16)], axis=1), (2, 1))

    n_layers, n_phys = cache_k.shape[:2]
    cache_k = cache_k.reshape(n_layers, n_phys, page * n_heads, hd)
    cache_v = cache_v.reshape(n_layers, n_phys, page * n_heads, hd)

    def page_spec(c):
        return pl.BlockSpec((None, None, page * n_heads, hd),
                            lambda b, p, pt: (layer, pt[b, n_pages - 1 - (p * npp + c)], 0, 0))

    grid_spec = pltpu.PrefetchScalarGridSpec(
        num_scalar_prefetch=1,
        grid=(bsz, n_pages // npp),
        in_specs=[pl.BlockSpec((n_heads, 1), lambda b, p, pt: (0, 0)),
                  pl.BlockSpec((None, n_heads, hd), lambda b, p, pt: (b, 0, 0))]
                 + [page_spec(c) for c in range(npp)] + [page_spec(c) for c in range(npp)]
                 + [pl.BlockSpec(scan_mat.shape, lambda b, p, pt: (0, 0))],
        out_specs=pl.BlockSpec((None, n_heads, hd), lambda b, p, pt: (b, 0, 0)),
        scratch_shapes=[pltpu.VMEM((n_heads // 2, n_heads, 2 * hd), BF16),
                        pltpu.VMEM((n_heads, 2 * hd), F32),
                        pltpu.VMEM((n_heads, LANES), F32)],
    )
    return pl.pallas_call(
        functools.partial(_decode_kernel, qscale=qscale),
        grid_spec=grid_spec,
        out_shape=jax.ShapeDtypeStruct((bsz, n_heads, hd), BF16),
        compiler_params=_params("arbitrary", "arbitrary"),
        name="attn_decode",
    )(page_table, nbias2, q, *([cache_k] * npp), *([cache_v] * npp), scan_mat)


def _group_map(pooled, wg_ref, sc_ref, o_ref):
    gd = wg_ref.shape[1]
    for g in range(len(POOL_WINDOWS)):
        lanes = slice(g * gd, (g + 1) * gd)
        h = jnp.dot(pooled[g].astype(BF16), wg_ref[g], preferred_element_type=F32)
        o_ref[:, lanes] = (h * sc_ref[:, lanes]).astype(o_ref.dtype)


def _pool_prompt_kernel(u_ref, prev_ref, wg_ref, sc_ref, o_ref, *, halo):
    i = pl.program_id(0)
    tm = u_ref.shape[0]
    gd = wg_ref.shape[1]
    prev = jnp.where(i > 0, prev_ref[...], 0.0)
    pos1 = i * tm + lax.broadcasted_iota(jnp.int32, (tm, 1), 0) + 1
    pooled = []
    for g, win in enumerate(POOL_WINDOWS):
        lanes = slice(g * gd, (g + 1) * gd)
        u = u_ref[:, lanes]
        ext = jnp.concatenate([prev[:, lanes], u], axis=0)
        span = 1
        while span < win:
            ext = ext + pltpu.roll(ext, span, 0)
            span *= 2
        cnt = jnp.minimum(pos1, win).astype(F32)
        pooled.append(ext[halo:] / cnt - u)
    _group_map(pooled, wg_ref, sc_ref, o_ref)


def _pool_prompt(u, wg, sc, *, tm=256):
    s, d = u.shape
    halo = 16
    gd = wg.shape[1]
    return pl.pallas_call(
        functools.partial(_pool_prompt_kernel, halo=halo),
        grid=(s // tm,),
        in_specs=[pl.BlockSpec((tm, d), lambda i: (i, 0)),
                  pl.BlockSpec((halo, d), lambda i: (jnp.maximum(i * (tm // halo) - 1, 0), 0)),
                  pl.BlockSpec((len(POOL_WINDOWS), gd, gd), lambda i: (0, 0, 0)),
                  pl.BlockSpec((1, d), lambda i: (0, 0))],
        out_specs=pl.BlockSpec((tm, d), lambda i: (i, 0)),
        out_shape=jax.ShapeDtypeStruct((s, d), BF16),
        compiler_params=_params("arbitrary"),
        name="pool_prompt",
    )(u, u, wg, sc)


def _pool_sample_kernel(u_ref, buf_ref, wg_ref, sc_ref, o_ref, *, pos1):
    gd = wg_ref.shape[1]
    d = u_ref.shape[1]
    nbuf = buf_ref.shape[1] // d
    pooled = []
    for g, win in enumerate(POOL_WINDOWS):
        lanes = slice(g * gd, (g + 1) * gd)
        u = u_ref[:, lanes]
        total = u
        for r in range(nbuf - (win - 1), nbuf):
            total = total + buf_ref[:, r * d + g * gd:r * d + (g + 1) * gd]
        pooled.append(total / float(min(pos1, win)) - u)
    _group_map(pooled, wg_ref, sc_ref, o_ref)


def _pool_sample(u, buf, wg, sc, *, pos1, tb=32):
    bsz, d = u.shape
    nbuf = buf.shape[1]
    gd = wg.shape[1]
    buf = buf.reshape(bsz, nbuf * d)
    return pl.pallas_call(
        functools.partial(_pool_sample_kernel, pos1=pos1),
        grid=(bsz // tb,),
        in_specs=[pl.BlockSpec((tb, d), lambda i: (i, 0)),
                  pl.BlockSpec((tb, nbuf * d), lambda i: (i, 0)),
                  pl.BlockSpec((len(POOL_WINDOWS), gd, gd), lambda i: (0, 0, 0)),
                  pl.BlockSpec((1, d), lambda i: (0, 0))],
        out_specs=pl.BlockSpec((tb, d), lambda i: (i, 0)),
        out_shape=jax.ShapeDtypeStruct((bsz, d), BF16),
        compiler_params=_params("arbitrary"),
        name="pool_sample",
    )(u, buf, wg, sc)


def kernel(x_prompt, x_sample, cache_k, cache_v, state_pool, page_table, ln_g, ln_b, w_qkv, attn_bias,
           w_attn_out, w_pool_in, w_pool_group, pool_scale, w_pool_out, w_mlp_up, w_mlp_down):
    batch, seq, d = x_prompt.shape
    dec_b, dec_t, _ = x_sample.shape
    depth = ln_g.shape[0]
    n_heads = d // HEAD_DIM
    page = cache_k.shape[2]
    past_len = page_table.shape[1] * page
    pool_buf = state_pool.shape[2]
    assert batch == 1 and dec_t == 1 and pool_buf == max(POOL_WINDOWS) - 1
    alpha = (2.0 * depth) ** 0.25
    attn_scale = HEAD_DIM ** -0.5

    yp = x_prompt.reshape(seq, d)
    ys = x_sample.reshape(dec_b, d)

    kp_l, vp_l, ks_l, vs_l, pp_l, ps_l = [], [], [], [], [], []
    for layer in range(depth):
        g0, b0 = ln_g[layer, 0].reshape(1, d), ln_b[layer, 0].reshape(1, d)
        g1, b1 = ln_g[layer, 1].reshape(1, d), ln_b[layer, 1].reshape(1, d)
        if layer % 2 == 0:
            a = layer // 2
            wq = w_qkv[a, :, :d].astype(BF16)
            wk = w_qkv[a, :, d:2 * d].astype(BF16)
            wv = w_qkv[a, :, 2 * d:].astype(BF16)
            wo = w_attn_out[a].astype(BF16)
            qscale = -attn_scale * LOG2E
            nb2 = -LOG2E * attn_bias[a]
            (qb,) = _matmul(yp, wq, (BF16,), scale=qscale)
            kf, kb = _matmul(yp, wk, (F32, BF16))
            vf, vb = _matmul(yp, wv, (F32, BF16))
            q_t = qb.reshape(seq, n_heads, HEAD_DIM).transpose(1, 2, 0)
            v_t = vb.reshape(seq // ATTN_TK, ATTN_TK, n_heads, HEAD_DIM).transpose(2, 0, 3, 1)
            mp = _attention_prompt(q_t, nb2, kb, v_t)
            kp_l.append(kf.reshape(batch, seq, n_heads, HEAD_DIM))
            vp_l.append(vf.reshape(batch, seq, n_heads, HEAD_DIM))
            (qs,) = _matmul(ys, wq, (F32,))
            (ksn,) = _matmul(ys, wk, (F32,))
            (vsn,) = _matmul(ys, wv, (F32,))
            ms = _attention_decode(qs.reshape(dec_b, n_heads, HEAD_DIM), cache_k, cache_v, page_table,
                                   nb2.reshape(n_heads, 1), layer=a, qscale=qscale).reshape(dec_b, d)
            ks_l.append(ksn.reshape(dec_b, dec_t, n_heads, HEAD_DIM))
            vs_l.append(vsn.reshape(dec_b, dec_t, n_heads, HEAD_DIM))
        else:
            p = layer // 2
            wi = w_pool_in[p].astype(BF16)
            wg = w_pool_group[p].astype(BF16)
            wo = w_pool_out[p].astype(BF16)
            sc = pool_scale[p].reshape(1, d)
            (up,) = _matmul(yp, wi, (F32,))
            mp = _pool_prompt(up, wg, sc)
            pp_l.append(up[seq - pool_buf:].reshape(batch, pool_buf, d))
            (us,) = _matmul(ys, wi, (F32,))
            ms = _pool_sample(us, state_pool[p], wg, sc, pos1=past_len + 1)
            ps_l.append(jnp.concatenate([state_pool[p][:, 1:], us[:, None, :]], axis=1))
        wu = w_mlp_up[layer].astype(BF16)
        wd = w_mlp_down[layer].astype(BF16)
        yp = _proj_ln(mp, wo, yp, g0, b0, alpha=alpha)
        ys = _proj_ln(ms, wo, ys, g0, b0, alpha=alpha)
        yp = _mlp(yp, wu, wd, g1, b1, alpha=alpha)
        ys = _mlp(ys, wu, wd, g1, b1, alpha=alpha)

    return (yp.reshape(batch, seq, d), ys.reshape(dec_b, dec_t, d),
            jnp.stack(kp_l), jnp.stack(vp_l), jnp.stack(pp_l),
            jnp.stack(ks_l), jnp.stack(vs_l), jnp.stack(ps_l))
```

```python
import functools
import math

import jax
import jax.numpy as jnp
from jax import lax
from jax.experimental import pallas as pl
from jax.experimental.pallas import tpu as pltpu

F32 = jnp.float32
BF16 = jnp.bfloat16

HEAD_DIM = 128
POOL_WINDOWS = (2, 4, 8, 16)
LN_EPS = 1e-5

LANES = 128
SUBLANES = 8
VMEM_LIMIT_BYTES = 56 * 1024 * 1024

ATTN_TQ = 512
ATTN_SUB = 128
ATTN_TK = ATTN_TQ
ATTN_HEADS_PER_STEP = 2
ATTN_TOTAL_ROWS = 16
DEC_PAGES_PER_STEP = 4

LOG2E = math.log2(math.e)
MASKED_LOG2 = -1e30


def _params(*semantics):
    return pltpu.CompilerParams(dimension_semantics=semantics, vmem_limit_bytes=VMEM_LIMIT_BYTES)


def _layer_norm(xf, g, b):
    mu = jnp.mean(xf, axis=-1, keepdims=True)
    xc = xf - mu
    var = jnp.mean(xc * xc, axis=-1, keepdims=True)
    return xc * lax.rsqrt(var + LN_EPS) * g + b


def _matmul_kernel(a_ref, b_ref, *o_refs, scale):
    acc = jnp.dot(a_ref[...].astype(BF16), b_ref[...], preferred_element_type=F32)
    for o_ref in o_refs:
        if o_ref.dtype == BF16 and scale is not None:
            o_ref[...] = (acc * scale).astype(BF16)
        else:
            o_ref[...] = acc.astype(o_ref.dtype)


def _matmul(a, b, out_dtypes, *, scale=None, tm=1024, tn=1024):
    m, k = a.shape
    n = b.shape[1]
    tm, tn = min(tm, m), min(tn, n)
    outs = pl.pallas_call(
        functools.partial(_matmul_kernel, scale=scale),
        grid=(m // tm, n // tn),
        in_specs=[pl.BlockSpec((tm, k), lambda i, j: (i, 0)),
                  pl.BlockSpec((k, tn), lambda i, j: (0, j))],
        out_specs=[pl.BlockSpec((tm, tn), lambda i, j: (i, j)) for _ in out_dtypes],
        out_shape=[jax.ShapeDtypeStruct((m, n), dt) for dt in out_dtypes],
        compiler_params=_params("arbitrary", "arbitrary"),
        name="matmul",
    )(a, b)
    return outs


def _proj_ln_kernel(a_ref, w_ref, x_ref, g_ref, b_ref, o_ref, *, alpha):
    m = jnp.dot(a_ref[...].astype(BF16), w_ref[...], preferred_element_type=F32)
    o_ref[...] = _layer_norm(alpha * x_ref[...] + m, g_ref[...], b_ref[...])


def _proj_ln(a, w, x, g, b, *, alpha, tm=256):
    m, k = a.shape
    d = w.shape[1]
    tm = min(tm, m)
    return pl.pallas_call(
        functools.partial(_proj_ln_kernel, alpha=alpha),
        grid=(m // tm,),
        in_specs=[pl.BlockSpec((tm, k), lambda i: (i, 0)),
                  pl.BlockSpec((k, d), lambda i: (0, 0)),
                  pl.BlockSpec((tm, d), lambda i: (i, 0)),
                  pl.BlockSpec((1, d), lambda i: (0, 0)),
                  pl.BlockSpec((1, d), lambda i: (0, 0))],
        out_specs=pl.BlockSpec((tm, d), lambda i: (i, 0)),
        out_shape=jax.ShapeDtypeStruct((m, d), F32),
        compiler_params=_params("arbitrary"),
        name="proj_ln",
    )(a, w, x, g, b)


def _mlp_kernel(x_ref, wu_ref, wd_ref, g_ref, b_ref, o_ref, acc_ref, xb_ref, *, alpha):
    kf = pl.program_id(1)

    @pl.when(kf == 0)
    def _():
        xb_ref[...] = x_ref[...].astype(BF16)
        acc_ref[...] = jnp.zeros_like(acc_ref)

    h = jnp.dot(xb_ref[...], wu_ref[...], preferred_element_type=F32)
    h = jnp.maximum(h, 0.0)
    acc_ref[...] += jnp.dot((h * h).astype(BF16), wd_ref[...], preferred_element_type=F32)

    @pl.when(kf == pl.num_programs(1) - 1)
    def _():
        o_ref[...] = _layer_norm(alpha * x_ref[...] + acc_ref[...], g_ref[...], b_ref[...])


def _mlp(x, wu, wd, g, b, *, alpha, tm=512, tf=1024):
    m, d = x.shape
    f = wu.shape[1]
    tm = min(tm, m)
    return pl.pallas_call(
        functools.partial(_mlp_kernel, alpha=alpha),
        grid=(m // tm, f // tf),
        in_specs=[pl.BlockSpec((tm, d), lambda i, kf: (i, 0)),
                  pl.BlockSpec((d, tf), lambda i, kf: (0, kf)),
                  pl.BlockSpec((tf, d), lambda i, kf: (kf, 0)),
                  pl.BlockSpec((1, d), lambda i, kf: (0, 0)),
                  pl.BlockSpec((1, d), lambda i, kf: (0, 0))],
        out_specs=pl.BlockSpec((tm, d), lambda i, kf: (i, 0)),
        out_shape=jax.ShapeDtypeStruct((m, d), F32),
        scratch_shapes=[pltpu.VMEM((tm, d), F32), pltpu.VMEM((tm, d), BF16)],
        compiler_params=_params("arbitrary", "arbitrary"),
        name="mlp",
    )(x, wu, wd, g, b)


def _log2_stay_and_beta(w):
    ls = jnp.minimum(w, 0.0) - jnp.log(1.0 + jnp.exp2(-jnp.abs(w))) * LOG2E
    return ls, ls - w


def _split_bf16(x):
    hi = x.astype(BF16)
    lo = (x - hi.astype(F32)).astype(BF16)
    return hi, lo


def _pad_left(x, width):
    missing = width - x.shape[1]
    return x if missing == 0 else jnp.concatenate([jnp.zeros((x.shape[0], missing), x.dtype), x], axis=1)


def _attn_block(k_ref, vt_ref, k_one, scan_mat, q_aug, j, state, diagonal):
    start = pl.multiple_of(j * ATTN_TK, ATTN_TK)
    heads = range(ATTN_HEADS_PER_STEP)
    subs = list(reversed(range(ATTN_TK // ATTN_SUB)))
    first_col = {sub: sub * ATTN_SUB if diagonal else 0 for sub in subs}
    scores = {}
    for hh in heads:
        for sub in subs:
            k_sub = k_ref[pl.ds(start + sub * ATTN_SUB, ATTN_SUB), hh * HEAD_DIM:(hh + 1) * HEAD_DIM]
            scores[hh, sub] = jnp.dot(jnp.concatenate([k_sub, k_one], axis=1), q_aug[hh][:, first_col[sub]:],
                                      preferred_element_type=F32)
    log_beta, log_stay = {}, {}
    for hh in heads:
        for sub in subs:
            ls, lb = _log2_stay_and_beta(scores[hh, sub])
            if diagonal:
                row = lax.broadcasted_iota(jnp.int32, ls.shape, 0)
                col = lax.broadcasted_iota(jnp.int32, ls.shape, 1)
                mask = row < col
                ls = jnp.where(mask, ls, 0.0)
                lb = jnp.where(mask, lb, MASKED_LOG2)
            log_beta[hh, sub] = lb
            log_stay[hh, sub] = ls.astype(BF16)
    scans = {key: jnp.dot(scan_mat, ls, preferred_element_type=F32) for key, ls in log_stay.items()}
    new_state = []
    for hh in heads:
        acc, carry = state[hh]
        weights = []
        for sub in subs:
            res = scans[hh, sub]
            a = jnp.exp2(log_beta[hh, sub] + res[:ATTN_SUB] + carry[:, first_col[sub]:])
            weights.append(_pad_left(a.astype(BF16), ATTN_TQ))
            carry = carry + _pad_left(res[ATTN_SUB:ATTN_SUB + 1], ATTN_TQ)
        a = jnp.concatenate(weights[::-1], axis=0)
        new_state.append((acc + jnp.dot(vt_ref[hh, j], a, preferred_element_type=F32), carry))
    return tuple(new_state)


def _attn_kernel(qt_ref, qbias_ref, k_ref, vt_ref, k_one_ref, scan_ref, o_ref):
    i = pl.program_id(1)
    scan_mat = scan_ref[...]
    k_one = k_one_ref[...]
    heads = range(ATTN_HEADS_PER_STEP)
    q_aug = [jnp.concatenate([qt_ref[hh], qbias_ref[hh]], axis=0) for hh in heads]

    def block(j, state, diagonal):
        return _attn_block(k_ref, vt_ref, k_one, scan_mat, q_aug, j, state, diagonal)

    zero = (jnp.zeros((HEAD_DIM, ATTN_TQ), F32), jnp.zeros((1, ATTN_TQ), F32))
    state = block(i, tuple(zero for _ in heads), True)
    state = lax.fori_loop(0, i, lambda t, s: block(i - 1 - t, s, False), state)
    for hh in heads:
        o_ref[:, hh * HEAD_DIM:(hh + 1) * HEAD_DIM] = state[hh][0].T.astype(o_ref.dtype)


def _attention_prompt(q_t, nbias2, k_b, v_t):
    n_heads, hd, s = q_t.shape
    hps = ATTN_HEADS_PER_STEP
    nb_hi = nbias2.astype(BF16)
    nb_lo = (nbias2 - nb_hi.astype(F32)).astype(BF16)
    q_bias = jnp.zeros((n_heads, hd, ATTN_TQ), BF16)
    q_bias = q_bias.at[:, 0, :].set(nb_hi[:, None]).at[:, 1, :].set(nb_lo[:, None])
    k_one = jnp.zeros((ATTN_SUB, hd), BF16).at[:, :2].set(1.0)
    tri = (jnp.arange(ATTN_SUB)[None, :] > jnp.arange(ATTN_SUB)[:, None]).astype(BF16)
    scan_mat = jnp.concatenate([tri, jnp.ones((ATTN_TOTAL_ROWS, ATTN_SUB), BF16)], axis=0)
    return pl.pallas_call(
        _attn_kernel,
        grid=(n_heads // hps, s // ATTN_TQ),
        in_specs=[pl.BlockSpec((hps, hd, ATTN_TQ), lambda h, i: (h, 0, i)),
                  pl.BlockSpec((hps, hd, ATTN_TQ), lambda h, i: (h, 0, 0)),
                  pl.BlockSpec((s, hps * hd), lambda h, i: (0, h)),
                  pl.BlockSpec((hps, s // ATTN_TK, hd, ATTN_TK), lambda h, i: (h, 0, 0, 0)),
                  pl.BlockSpec(k_one.shape, lambda h, i: (0, 0)),
                  pl.BlockSpec(scan_mat.shape, lambda h, i: (0, 0))],
        out_specs=pl.BlockSpec((ATTN_TQ, hps * hd), lambda h, i: (i, h)),
        out_shape=jax.ShapeDtypeStruct((s, n_heads * hd), BF16),
        compiler_params=_params("arbitrary", "arbitrary"),
        name="attn_prompt",
    )(q_t, q_bias, k_b, v_t, k_one, scan_mat)


def _head_rows(block_ref, h):
    page, tile_heads, hd = block_ref.shape
    return block_ref.reshape(page * tile_heads, hd)[pl.ds(h, page, stride=tile_heads), :]


def _decode_kernel(pt_ref, nb_ref, q_ref, *refs, qscale):
    npp = DEC_PAGES_PER_STEP
    n_heads, hd = q_ref.shape
    page, tile_heads, _ = refs[0].shape
    n_tiles = n_heads // tile_heads
    n_blocks = npp * n_tiles
    k_refs, v_refs = refs[:n_blocks], refs[n_blocks:2 * n_blocks]
    scan_ref, o_ref, q2_ref, acc_ref, carry_ref = refs[2 * n_blocks:]
    p = pl.program_id(1)
    n_pairs = n_heads // 2
    row = lax.broadcasted_iota(jnp.int32, (n_heads, 2 * hd), 0)
    col = lax.broadcasted_iota(jnp.int32, (n_heads, 2 * hd), 1)
    own_half = (row % 2) == (col // hd)
    row_tok = lax.broadcasted_iota(jnp.int32, (n_heads, 2 * page), 0)

    @pl.when(p == 0)
    def _():
        qs = q_ref[...] * qscale
        qq = jnp.concatenate([qs, qs], axis=1)
        for pr in range(n_pairs):
            q2_ref[pr] = jnp.where((row // 2 == pr) & own_half, qq, 0.0).astype(BF16)
        acc_ref[...] = jnp.zeros_like(acc_ref)
        carry_ref[...] = jnp.zeros_like(carry_ref)

    def head_rows(page_refs, pg, h):
        return _head_rows(page_refs[pg * n_tiles + h // tile_heads], h % tile_heads)

    def pair_tile(page_refs, older, newer, pr):
        rows = [jnp.concatenate([head_rows(page_refs, pg, 2 * pr), head_rows(page_refs, pg, 2 * pr + 1)], axis=1)
                for pg in (older, newer)]
        return jnp.concatenate(rows, axis=0).astype(BF16)

    nb = nb_ref[...]
    acc = acc_ref[...]
    carry = carry_ref[...]
    page_pairs = [(c + 1, c) for c in range(0, npp, 2)]
    scores = []
    for older, newer in page_pairs:
        z = None
        for pr in range(n_pairs):
            zp = lax.dot_general(q2_ref[pr], pair_tile(k_refs, older, newer, pr), (((1,), (1,)), ((), ())),
                                 preferred_element_type=F32)
            z = zp if z is None else z + zp
        scores.append(z)
    log_beta, scans = [], []
    for z in scores:
        ls, lb = _log2_stay_and_beta(z + nb)
        hi, lo = _split_bf16(ls)
        log_beta.append(lb)
        scans.append(jnp.dot(jnp.concatenate([hi, lo], axis=1), scan_ref[...], preferred_element_type=F32))
    weights = []
    for lb, scan in zip(log_beta, scans):
        weights.append(jnp.exp2(lb + scan[:, :2 * page] + jnp.concatenate([carry] * (2 * page // LANES), axis=1)))
        carry = carry + scan[:, 2 * page:]
    for (older, newer), a in zip(page_pairs, weights):
        for pr in range(n_pairs):
            a_pr = jnp.where(row_tok // 2 == pr, a, 0.0).astype(BF16)
            acc = acc + jnp.dot(a_pr, pair_tile(v_refs, older, newer, pr), preferred_element_type=F32)
    acc_ref[...] = acc
    carry_ref[...] = carry

    @pl.when(p == pl.num_programs(1) - 1)
    def _():
        o_ref[...] = jnp.where(row[:, :hd] % 2 == 0, acc[:, :hd], acc[:, hd:]).astype(o_ref.dtype)


def _attention_decode(q, cache_k, cache_v, page_table, nbias2, *, layer, qscale):
    bsz, n_heads, hd = q.shape
    page = cache_k.shape[2]
    n_pages = page_table.shape[1]
    npp = DEC_PAGES_PER_STEP
    assert npp % 2 == 0 and n_pages % npp == 0 and n_heads % 2 == 0
    tri = (jnp.arange(2 * page)[:, None] > jnp.arange(2 * page)[None, :]).astype(BF16)
    scan_mat = jnp.tile(jnp.concatenate([tri, jnp.ones((2 * page, LANES), BF16)], axis=1), (2, 1))

    tile_heads = SUBLANES if n_heads % SUBLANES == 0 else n_heads
    n_tiles = n_heads // tile_heads

    def page_spec(c, t):
        return pl.BlockSpec((None, None, page, tile_heads, hd),
                            lambda b, p, pt: (layer, pt[b, n_pages - 1 - (p * npp + c)], 0, t, 0))

    page_specs = [page_spec(c, t) for c in range(npp) for t in range(n_tiles)]
    grid_spec = pltpu.PrefetchScalarGridSpec(
        num_scalar_prefetch=1,
        grid=(bsz, n_pages // npp),
        in_specs=[pl.BlockSpec((n_heads, 1), lambda b, p, pt: (0, 0)),
                  pl.BlockSpec((None, n_heads, hd), lambda b, p, pt: (b, 0, 0))]
                 + page_specs + page_specs
                 + [pl.BlockSpec(scan_mat.shape, lambda b, p, pt: (0, 0))],
        out_specs=pl.BlockSpec((None, n_heads, hd), lambda b, p, pt: (b, 0, 0)),
        scratch_shapes=[pltpu.VMEM((n_heads // 2, n_heads, 2 * hd), BF16),
                        pltpu.VMEM((n_heads, 2 * hd), F32),
                        pltpu.VMEM((n_heads, LANES), F32)],
    )
    return pl.pallas_call(
        functools.partial(_decode_kernel, qscale=qscale),
        grid_spec=grid_spec,
        out_shape=jax.ShapeDtypeStruct((bsz, n_heads, hd), BF16),
        compiler_params=_params("arbitrary", "arbitrary"),
        name="attn_decode",
    )(page_table, nbias2, q, *([cache_k] * len(page_specs)), *([cache_v] * len(page_specs)), scan_mat)


def _group_map(pooled, wg_ref, sc_ref, o_ref):
    gd = wg_ref.shape[1]
    for g in range(len(POOL_WINDOWS)):
        lanes = slice(g * gd, (g + 1) * gd)
        h = jnp.dot(pooled[g].astype(BF16), wg_ref[g], preferred_element_type=F32)
        o_ref[:, lanes] = (h * sc_ref[:, lanes]).astype(o_ref.dtype)


def _pool_prompt_kernel(u_ref, prev_ref, wg_ref, sc_ref, o_ref, *, halo):
    i = pl.program_id(0)
    tm = u_ref.shape[0]
    gd = wg_ref.shape[1]
    prev = jnp.where(i > 0, prev_ref[...], 0.0)
    pos1 = i * tm + lax.broadcasted_iota(jnp.int32, (tm, 1), 0) + 1
    pooled = []
    for g, win in enumerate(POOL_WINDOWS):
        lanes = slice(g * gd, (g + 1) * gd)
        u = u_ref[:, lanes]
        ext = jnp.concatenate([prev[:, lanes], u], axis=0)
        span = 1
        while span < win:
            ext = ext + pltpu.roll(ext, span, 0)
            span *= 2
        cnt = jnp.minimum(pos1, win).astype(F32)
        pooled.append(ext[halo:] / cnt - u)
    _group_map(pooled, wg_ref, sc_ref, o_ref)


def _pool_prompt(u, wg, sc, *, tm=256):
    s, d = u.shape
    halo = 16
    gd = wg.shape[1]
    return pl.pallas_call(
        functools.partial(_pool_prompt_kernel, halo=halo),
        grid=(s // tm,),
        in_specs=[pl.BlockSpec((tm, d), lambda i: (i, 0)),
                  pl.BlockSpec((halo, d), lambda i: (jnp.maximum(i * (tm // halo) - 1, 0), 0)),
                  pl.BlockSpec((len(POOL_WINDOWS), gd, gd), lambda i: (0, 0, 0)),
                  pl.BlockSpec((1, d), lambda i: (0, 0))],
        out_specs=pl.BlockSpec((tm, d), lambda i: (i, 0)),
        out_shape=jax.ShapeDtypeStruct((s, d), BF16),
        compiler_params=_params("arbitrary"),
        name="pool_prompt",
    )(u, u, wg, sc)


def _pool_sample_kernel(u_ref, buf_ref, wg_ref, sc_ref, o_ref, *, pos1):
    gd = wg_ref.shape[1]
    d = u_ref.shape[1]
    nbuf = buf_ref.shape[1] // d
    pooled = []
    for g, win in enumerate(POOL_WINDOWS):
        lanes = slice(g * gd, (g + 1) * gd)
        u = u_ref[:, lanes]
        total = u
        for r in range(nbuf - (win - 1), nbuf):
            total = total + buf_ref[:, r * d + g * gd:r * d + (g + 1) * gd]
        pooled.append(total / float(min(pos1, win)) - u)
    _group_map(pooled, wg_ref, sc_ref, o_ref)


def _pool_sample(u, buf, wg, sc, *, pos1, tb=32):
    bsz, d = u.shape
    nbuf = buf.shape[1]
    gd = wg.shape[1]
    buf = buf.reshape(bsz, nbuf * d)
    return pl.pallas_call(
        functools.partial(_pool_sample_kernel, pos1=pos1),
        grid=(bsz // tb,),
        in_specs=[pl.BlockSpec((tb, d), lambda i: (i, 0)),
                  pl.BlockSpec((tb, nbuf * d), lambda i: (i, 0)),
                  pl.BlockSpec((len(POOL_WINDOWS), gd, gd), lambda i: (0, 0, 0)),
                  pl.BlockSpec((1, d), lambda i: (0, 0))],
        out_specs=pl.BlockSpec((tb, d), lambda i: (i, 0)),
        out_shape=jax.ShapeDtypeStruct((bsz, d), BF16),
        compiler_params=_params("arbitrary"),
        name="pool_sample",
    )(u, buf, wg, sc)


def kernel(x_prompt, x_sample, cache_k, cache_v, state_pool, page_table, ln_g, ln_b, w_qkv, attn_bias,
           w_attn_out, w_pool_in, w_pool_group, pool_scale, w_pool_out, w_mlp_up, w_mlp_down):
    batch, seq, d = x_prompt.shape
    dec_b, dec_t, _ = x_sample.shape
    depth = ln_g.shape[0]
    n_heads = d // HEAD_DIM
    page = cache_k.shape[2]
    past_len = page_table.shape[1] * page
    pool_buf = state_pool.shape[2]
    assert batch == 1 and dec_t == 1 and pool_buf == max(POOL_WINDOWS) - 1
    alpha = (2.0 * depth) ** 0.25
    attn_scale = HEAD_DIM ** -0.5

    yp = x_prompt.reshape(seq, d)
    ys = x_sample.reshape(dec_b, d)

    kp_l, vp_l, ks_l, vs_l, pp_l, ps_l = [], [], [], [], [], []
    for layer in range(depth):
        g0, b0 = ln_g[layer, 0].reshape(1, d), ln_b[layer, 0].reshape(1, d)
        g1, b1 = ln_g[layer, 1].reshape(1, d), ln_b[layer, 1].reshape(1, d)
        if layer % 2 == 0:
            a = layer // 2
            wq = w_qkv[a, :, :d].astype(BF16)
            wk = w_qkv[a, :, d:2 * d].astype(BF16)
            wv = w_qkv[a, :, 2 * d:].astype(BF16)
            wo = w_attn_out[a].astype(BF16)
            qscale = -attn_scale * LOG2E
            nb2 = -LOG2E * attn_bias[a]
            (qb,) = _matmul(yp, wq, (BF16,), scale=qscale)
            kf, kb = _matmul(yp, wk, (F32, BF16))
            vf, vb = _matmul(yp, wv, (F32, BF16))
            q_t = qb.reshape(seq, n_heads, HEAD_DIM).transpose(1, 2, 0)
            v_t = vb.reshape(seq // ATTN_TK, ATTN_TK, n_heads, HEAD_DIM).transpose(2, 0, 3, 1)
            mp = _attention_prompt(q_t, nb2, kb, v_t)
            kp_l.append(kf.reshape(batch, seq, n_heads, HEAD_DIM))
            vp_l.append(vf.reshape(batch, seq, n_heads, HEAD_DIM))
            (qs,) = _matmul(ys, wq, (F32,))
            (ksn,) = _matmul(ys, wk, (F32,))
            (vsn,) = _matmul(ys, wv, (F32,))
            ms = _attention_decode(qs.reshape(dec_b, n_heads, HEAD_DIM), cache_k, cache_v, page_table,
                                   nb2.reshape(n_heads, 1), layer=a, qscale=qscale).reshape(dec_b, d)
            ks_l.append(ksn.reshape(dec_b, dec_t, n_heads, HEAD_DIM))
            vs_l.append(vsn.reshape(dec_b, dec_t, n_heads, HEAD_DIM))
        else:
            p = layer // 2
            wi = w_pool_in[p].astype(BF16)
            wg = w_pool_group[p].astype(BF16)
            wo = w_pool_out[p].astype(BF16)
            sc = pool_scale[p].reshape(1, d)
            (up,) = _matmul(yp, wi, (F32,))
            mp = _pool_prompt(up, wg, sc)
            pp_l.append(up[seq - pool_buf:].reshape(batch, pool_buf, d))
            (us,) = _matmul(ys, wi, (F32,))
            ms = _pool_sample(us, state_pool[p], wg, sc, pos1=past_len + 1)
            ps_l.append(jnp.concatenate([state_pool[p][:, 1:], us[:, None, :]], axis=1))
        wu = w_mlp_up[layer].astype(BF16)
        wd = w_mlp_down[layer].astype(BF16)
        yp = _proj_ln(mp, wo, yp, g0, b0, alpha=alpha)
        ys = _proj_ln(ms, wo, ys, g0, b0, alpha=alpha)
        yp = _mlp(yp, wu, wd, g1, b1, alpha=alpha)
        ys = _mlp(ys, wu, wd, g1, b1, alpha=alpha)

    return (yp.reshape(batch, seq, d), ys.reshape(dec_b, dec_t, d),
            jnp.stack(kp_l), jnp.stack(vp_l), jnp.stack(pp_l),
            jnp.stack(ks_l), jnp.stack(vs_l), jnp.stack(ps_l))
```

```python
import functools
import math

import jax
import jax.numpy as jnp
from jax import lax
from jax.experimental import pallas as pl
from jax.experimental.pallas import tpu as pltpu

F32 = jnp.float32
BF16 = jnp.bfloat16

HEAD_DIM = 128
POOL_WINDOWS = (2, 4, 8, 16)
LN_EPS = 1e-5

LANES = 128
SUBLANES = 8
VMEM_LIMIT_BYTES = 56 * 1024 * 1024

ATTN_TQ = 512
ATTN_SUB = 128
ATTN_TK = ATTN_TQ
ATTN_HEADS_PER_STEP = 2
ATTN_TOTAL_ROWS = 16
DEC_PAGES_PER_STEP = 4

LOG2E = math.log2(math.e)
MASKED_LOG2 = -1e30


def _params(*semantics):
    return pltpu.CompilerParams(dimension_semantics=semantics, vmem_limit_bytes=VMEM_LIMIT_BYTES)


def _layer_norm(xf, g, b):
    mu = jnp.mean(xf, axis=-1, keepdims=True)
    xc = xf - mu
    var = jnp.mean(xc * xc, axis=-1, keepdims=True)
    return xc * lax.rsqrt(var + LN_EPS) * g + b


def _matmul_kernel(a_ref, b_ref, *o_refs, scale):
    acc = jnp.dot(a_ref[...].astype(BF16), b_ref[...], preferred_element_type=F32)
    for o_ref in o_refs:
        if o_ref.dtype == BF16 and scale is not None:
            o_ref[...] = (acc * scale).astype(BF16)
        else:
            o_ref[...] = acc.astype(o_ref.dtype)


def _matmul(a, b, out_dtypes, *, scale=None, tm=1024, tn=1024):
    m, k = a.shape
    n = b.shape[1]
    tm, tn = min(tm, m), min(tn, n)
    outs = pl.pallas_call(
        functools.partial(_matmul_kernel, scale=scale),
        grid=(m // tm, n // tn),
        in_specs=[pl.BlockSpec((tm, k), lambda i, j: (i, 0)),
                  pl.BlockSpec((k, tn), lambda i, j: (0, j))],
        out_specs=[pl.BlockSpec((tm, tn), lambda i, j: (i, j)) for _ in out_dtypes],
        out_shape=[jax.ShapeDtypeStruct((m, n), dt) for dt in out_dtypes],
        compiler_params=_params("arbitrary", "arbitrary"),
        name="matmul",
    )(a, b)
    return outs


def _store_head_rows(o_ref, h, val):
    rows, tile_heads, hd = o_ref.shape
    o_ref.reshape(rows * tile_heads, hd)[pl.ds(h, rows, stride=tile_heads), :] = val


def _kv_kernel(x_ref, w_ref, f_ref, b_ref):
    acc = jnp.dot(x_ref[...].astype(BF16), w_ref[...], preferred_element_type=F32)
    b_ref[...] = acc.astype(BF16)
    hd = f_ref.shape[-1]
    for h in range(f_ref.shape[1]):
        _store_head_rows(f_ref, h, acc[:, h * hd:(h + 1) * hd])


def _kv_proj(x, w, *, tm=1024):
    m, d = x.shape
    n = w.shape[1]
    n_heads = n // HEAD_DIM
    tile_heads = SUBLANES if n_heads % SUBLANES == 0 else n_heads
    tn = tile_heads * HEAD_DIM
    tm = min(tm, m)
    return pl.pallas_call(
        _kv_kernel,
        grid=(m // tm, n // tn),
        in_specs=[pl.BlockSpec((tm, d), lambda i, j: (i, 0)),
                  pl.BlockSpec((d, tn), lambda i, j: (0, j))],
        out_specs=[pl.BlockSpec((tm, tile_heads, HEAD_DIM), lambda i, j: (i, j, 0)),
                   pl.BlockSpec((tm, tn), lambda i, j: (i, j))],
        out_shape=[jax.ShapeDtypeStruct((m, n_heads, HEAD_DIM), F32), jax.ShapeDtypeStruct((m, n), BF16)],
        compiler_params=_params("arbitrary", "arbitrary"),
        name="kv_proj",
    )(x, w)


def _proj_ln_kernel(a_ref, w_ref, x_ref, g_ref, b_ref, o_ref, *, alpha):
    m = jnp.dot(a_ref[...].astype(BF16), w_ref[...], preferred_element_type=F32)
    o_ref[...] = _layer_norm(alpha * x_ref[...] + m, g_ref[...], b_ref[...])


def _proj_ln(a, w, x, g, b, *, alpha, tm=512):
    m, k = a.shape
    d = w.shape[1]
    tm = min(tm, m)
    return pl.pallas_call(
        functools.partial(_proj_ln_kernel, alpha=alpha),
        grid=(m // tm,),
        in_specs=[pl.BlockSpec((tm, k), lambda i: (i, 0)),
                  pl.BlockSpec((k, d), lambda i: (0, 0)),
                  pl.BlockSpec((tm, d), lambda i: (i, 0)),
                  pl.BlockSpec((1, d), lambda i: (0, 0)),
                  pl.BlockSpec((1, d), lambda i: (0, 0))],
        out_specs=pl.BlockSpec((tm, d), lambda i: (i, 0)),
        out_shape=jax.ShapeDtypeStruct((m, d), F32),
        compiler_params=_params("arbitrary"),
        name="proj_ln",
    )(a, w, x, g, b)


def _mlp_step(kf, n_kf, x_ref, wu, wd, g_ref, b_ref, o_ref, acc_ref, xb_ref, alpha):
    @pl.when(kf == 0)
    def _():
        xb_ref[...] = x_ref[...].astype(BF16)
        acc_ref[...] = jnp.zeros_like(acc_ref)

    h = jnp.dot(xb_ref[...], wu, preferred_element_type=F32)
    h = jnp.maximum(h, 0.0)
    acc_ref[...] += jnp.dot((h * h).astype(BF16), wd, preferred_element_type=F32)

    @pl.when(kf == n_kf - 1)
    def _():
        o_ref[...] = _layer_norm(alpha * x_ref[...] + acc_ref[...], g_ref[...], b_ref[...])


def _mlp_kernel(x_ref, wu_ref, wd_ref, g_ref, b_ref, o_ref, acc_ref, xb_ref, *, alpha):
    _mlp_step(pl.program_id(1), pl.num_programs(1), x_ref, wu_ref[...], wd_ref[...], g_ref, b_ref, o_ref,
              acc_ref, xb_ref, alpha)


def _mlp_cast_kernel(x_ref, wu_ref, wd_ref, g_ref, b_ref, o_ref, wub_ref, wdb_ref, acc_ref, xb_ref, *, alpha):
    wu = wu_ref[...].astype(BF16)
    wd = wd_ref[...].astype(BF16)
    wub_ref[...] = wu
    wdb_ref[...] = wd
    _mlp_step(pl.program_id(0), pl.num_programs(0), x_ref, wu, wd, g_ref, b_ref, o_ref, acc_ref, xb_ref, alpha)


def _mlp_cast(x, w_up, w_down, layer, g, b, *, alpha, tf=512):
    m, d = x.shape
    f = w_up.shape[2]
    return pl.pallas_call(
        functools.partial(_mlp_cast_kernel, alpha=alpha),
        grid=(f // tf,),
        in_specs=[pl.BlockSpec((m, d), lambda kf: (0, 0)),
                  pl.BlockSpec((None, d, tf), lambda kf: (layer, 0, kf)),
                  pl.BlockSpec((None, tf, d), lambda kf: (layer, kf, 0)),
                  pl.BlockSpec((1, d), lambda kf: (0, 0)),
                  pl.BlockSpec((1, d), lambda kf: (0, 0))],
        out_specs=[pl.BlockSpec((m, d), lambda kf: (0, 0)),
                   pl.BlockSpec((d, tf), lambda kf: (0, kf)),
                   pl.BlockSpec((tf, d), lambda kf: (kf, 0))],
        out_shape=[jax.ShapeDtypeStruct((m, d), F32), jax.ShapeDtypeStruct((d, f), BF16),
                   jax.ShapeDtypeStruct((f, d), BF16)],
        scratch_shapes=[pltpu.VMEM((m, d), F32), pltpu.VMEM((m, d), BF16)],
        compiler_params=_params("arbitrary"),
        name="mlp_cast",
    )(x, w_up, w_down, g, b)


def _mlp(x, wu, wd, g, b, *, alpha, tm=512, tf=1024):
    m, d = x.shape
    f = wu.shape[1]
    tm = min(tm, m)
    return pl.pallas_call(
        functools.partial(_mlp_kernel, alpha=alpha),
        grid=(m // tm, f // tf),
        in_specs=[pl.BlockSpec((tm, d), lambda i, kf: (i, 0)),
                  pl.BlockSpec((d, tf), lambda i, kf: (0, kf)),
                  pl.BlockSpec((tf, d), lambda i, kf: (kf, 0)),
                  pl.BlockSpec((1, d), lambda i, kf: (0, 0)),
                  pl.BlockSpec((1, d), lambda i, kf: (0, 0))],
        out_specs=pl.BlockSpec((tm, d), lambda i, kf: (i, 0)),
        out_shape=jax.ShapeDtypeStruct((m, d), F32),
        scratch_shapes=[pltpu.VMEM((tm, d), F32), pltpu.VMEM((tm, d), BF16)],
        compiler_params=_params("arbitrary", "arbitrary"),
        name="mlp",
    )(x, wu, wd, g, b)


def _log2_stay_and_beta(w):
    ls = jnp.minimum(w, 0.0) - jnp.log(1.0 + jnp.exp2(-jnp.abs(w))) * LOG2E
    return ls, ls - w


def _split_bf16(x):
    hi = x.astype(BF16)
    lo = (x - hi.astype(F32)).astype(BF16)
    return hi, lo


def _pad_left(x, width):
    missing = width - x.shape[1]
    return x if missing == 0 else jnp.concatenate([jnp.zeros((x.shape[0], missing), x.dtype), x], axis=1)


def _attn_block(k_ref, vt_ref, k_one, scan_mat, q_aug, j, state, diagonal):
    start = pl.multiple_of(j * ATTN_TK, ATTN_TK)
    heads = range(ATTN_HEADS_PER_STEP)
    subs = list(reversed(range(ATTN_TK // ATTN_SUB)))
    first_col = {sub: sub * ATTN_SUB if diagonal else 0 for sub in subs}
    scores = {}
    for hh in heads:
        for sub in subs:
            k_sub = k_ref[pl.ds(start + sub * ATTN_SUB, ATTN_SUB), hh * HEAD_DIM:(hh + 1) * HEAD_DIM]
            scores[hh, sub] = jnp.dot(jnp.concatenate([k_sub, k_one], axis=1), q_aug[hh][:, first_col[sub]:],
                                      preferred_element_type=F32)
    log_beta, log_stay = {}, {}
    for hh in heads:
        for sub in subs:
            ls, lb = _log2_stay_and_beta(scores[hh, sub])
            if diagonal:
                row = lax.broadcasted_iota(jnp.int32, ls.shape, 0)
                col = lax.broadcasted_iota(jnp.int32, ls.shape, 1)
                mask = row < col
                ls = jnp.where(mask, ls, 0.0)
                lb = jnp.where(mask, lb, MASKED_LOG2)
            log_beta[hh, sub] = lb
            log_stay[hh, sub] = ls.astype(BF16)
    scans = {key: jnp.dot(scan_mat, ls, preferred_element_type=F32) for key, ls in log_stay.items()}
    new_state = []
    for hh in heads:
        acc, carry = state[hh]
        weights = []
        for sub in subs:
            res = scans[hh, sub]
            a = jnp.exp2(log_beta[hh, sub] + res[:ATTN_SUB] + carry[:, first_col[sub]:])
            weights.append(_pad_left(a.astype(BF16), ATTN_TQ))
            carry = carry + _pad_left(res[ATTN_SUB:ATTN_SUB + 1], ATTN_TQ)
        a = jnp.concatenate(weights[::-1], axis=0)
        new_state.append((acc + jnp.dot(vt_ref[hh, j], a, preferred_element_type=F32), carry))
    return tuple(new_state)


def _attn_kernel(qt_ref, qbias_ref, k_ref, vt_ref, k_one_ref, scan_ref, o_ref):
    i = pl.program_id(1)
    scan_mat = scan_ref[...]
    k_one = k_one_ref[...]
    heads = range(ATTN_HEADS_PER_STEP)
    q_aug = [jnp.concatenate([qt_ref[hh], qbias_ref[hh]], axis=0) for hh in heads]

    def block(j, state, diagonal):
        return _attn_block(k_ref, vt_ref, k_one, scan_mat, q_aug, j, state, diagonal)

    zero = (jnp.zeros((HEAD_DIM, ATTN_TQ), F32), jnp.zeros((1, ATTN_TQ), F32))
    state = block(i, tuple(zero for _ in heads), True)
    state = lax.fori_loop(0, i, lambda t, s: block(i - 1 - t, s, False), state)
    for hh in heads:
        o_ref[:, hh * HEAD_DIM:(hh + 1) * HEAD_DIM] = state[hh][0].T.astype(o_ref.dtype)


def _attention_prompt(q_t, nbias2, k_b, v_t):
    n_heads, hd, s = q_t.shape
    hps = ATTN_HEADS_PER_STEP
    nb_hi = nbias2.astype(BF16)
    nb_lo = (nbias2 - nb_hi.astype(F32)).astype(BF16)
    q_bias = jnp.zeros((n_heads, hd, ATTN_TQ), BF16)
    q_bias = q_bias.at[:, 0, :].set(nb_hi[:, None]).at[:, 1, :].set(nb_lo[:, None])
    k_one = jnp.zeros((ATTN_SUB, hd), BF16).at[:, :2].set(1.0)
    tri = (jnp.arange(ATTN_SUB)[None, :] > jnp.arange(ATTN_SUB)[:, None]).astype(BF16)
    scan_mat = jnp.concatenate([tri, jnp.ones((ATTN_TOTAL_ROWS, ATTN_SUB), BF16)], axis=0)
    return pl.pallas_call(
        _attn_kernel,
        grid=(n_heads // hps, s // ATTN_TQ),
        in_specs=[pl.BlockSpec((hps, hd, ATTN_TQ), lambda h, i: (h, 0, i)),
                  pl.BlockSpec((hps, hd, ATTN_TQ), lambda h, i: (h, 0, 0)),
                  pl.BlockSpec((s, hps * hd), lambda h, i: (0, h)),
                  pl.BlockSpec((hps, s // ATTN_TK, hd, ATTN_TK), lambda h, i: (h, 0, 0, 0)),
                  pl.BlockSpec(k_one.shape, lambda h, i: (0, 0)),
                  pl.BlockSpec(scan_mat.shape, lambda h, i: (0, 0))],
        out_specs=pl.BlockSpec((ATTN_TQ, hps * hd), lambda h, i: (i, h)),
        out_shape=jax.ShapeDtypeStruct((s, n_heads * hd), BF16),
        compiler_params=_params("arbitrary", "arbitrary"),
        name="attn_prompt",
    )(q_t, q_bias, k_b, v_t, k_one, scan_mat)


def _head_rows(block_ref, h):
    page, tile_heads, hd = block_ref.shape
    return block_ref.reshape(page * tile_heads, hd)[pl.ds(h, page, stride=tile_heads), :]


def _decode_kernel(pt_ref, nb_ref, q_ref, *refs, qscale):
    npp = DEC_PAGES_PER_STEP
    n_heads, hd = q_ref.shape
    page, tile_heads, _ = refs[0].shape
    n_tiles = n_heads // tile_heads
    n_blocks = npp * n_tiles
    k_refs, v_refs = refs[:n_blocks], refs[n_blocks:2 * n_blocks]
    scan_ref, o_ref, q2_ref, acc_ref, carry_ref = refs[2 * n_blocks:]
    p = pl.program_id(1)
    n_pairs = n_heads // 2
    row = lax.broadcasted_iota(jnp.int32, (n_heads, 2 * hd), 0)
    col = lax.broadcasted_iota(jnp.int32, (n_heads, 2 * hd), 1)
    own_half = (row % 2) == (col // hd)
    row_tok = lax.broadcasted_iota(jnp.int32, (n_heads, 2 * page), 0)

    @pl.when(p == 0)
    def _():
        qs = q_ref[...] * qscale
        qq = jnp.concatenate([qs, qs], axis=1)
        for pr in range(n_pairs):
            q2_ref[pr] = jnp.where((row // 2 == pr) & own_half, qq, 0.0).astype(BF16)
        acc_ref[...] = jnp.zeros_like(acc_ref)
        carry_ref[...] = jnp.zeros_like(carry_ref)

    def head_rows(page_refs, pg, h):
        return _head_rows(page_refs[pg * n_tiles + h // tile_heads], h % tile_heads)

    def pair_tile(page_refs, older, newer, pr):
        rows = [jnp.concatenate([head_rows(page_refs, pg, 2 * pr), head_rows(page_refs, pg, 2 * pr + 1)], axis=1)
                for pg in (older, newer)]
        return jnp.concatenate(rows, axis=0).astype(BF16)

    nb = nb_ref[...]
    acc = acc_ref[...]
    carry = carry_ref[...]
    page_pairs = [(c + 1, c) for c in range(0, npp, 2)]
    scores = []
    for older, newer in page_pairs:
        z = None
        for pr in range(n_pairs):
            zp = lax.dot_general(q2_ref[pr], pair_tile(k_refs, older, newer, pr), (((1,), (1,)), ((), ())),
                                 preferred_element_type=F32)
            z = zp if z is None else z + zp
        scores.append(z)
    log_beta, scans = [], []
    for z in scores:
        ls, lb = _log2_stay_and_beta(z + nb)
        hi, lo = _split_bf16(ls)
        log_beta.append(lb)
        scans.append(jnp.dot(jnp.concatenate([hi, lo], axis=1), scan_ref[...], preferred_element_type=F32))
    weights = []
    for lb, scan in zip(log_beta, scans):
        weights.append(jnp.exp2(lb + scan[:, :2 * page] + jnp.concatenate([carry] * (2 * page // LANES), axis=1)))
        carry = carry + scan[:, 2 * page:]
    for (older, newer), a in zip(page_pairs, weights):
        for pr in range(n_pairs):
            a_pr = jnp.where(row_tok // 2 == pr, a, 0.0).astype(BF16)
            acc = acc + jnp.dot(a_pr, pair_tile(v_refs, older, newer, pr), preferred_element_type=F32)
    acc_ref[...] = acc
    carry_ref[...] = carry

    @pl.when(p == pl.num_programs(1) - 1)
    def _():
        o_ref[...] = jnp.where(row[:, :hd] % 2 == 0, acc[:, :hd], acc[:, hd:]).astype(o_ref.dtype)


def _attention_decode(q, cache_k, cache_v, page_table, nbias2, *, layer, qscale):
    bsz, n_heads, hd = q.shape
    page = cache_k.shape[2]
    n_pages = page_table.shape[1]
    npp = DEC_PAGES_PER_STEP
    assert npp % 2 == 0 and n_pages % npp == 0 and n_heads % 2 == 0
    tri = (jnp.arange(2 * page)[:, None] > jnp.arange(2 * page)[None, :]).astype(BF16)
    scan_mat = jnp.tile(jnp.concatenate([tri, jnp.ones((2 * page, LANES), BF16)], axis=1), (2, 1))

    tile_heads = SUBLANES if n_heads % SUBLANES == 0 else n_heads
    n_tiles = n_heads // tile_heads

    def page_spec(c, t):
        return pl.BlockSpec((None, None, page, tile_heads, hd),
                            lambda b, p, pt: (layer, pt[b, n_pages - 1 - (p * npp + c)], 0, t, 0))

    page_specs = [page_spec(c, t) for c in range(npp) for t in range(n_tiles)]
    grid_spec = pltpu.PrefetchScalarGridSpec(
        num_scalar_prefetch=1,
        grid=(bsz, n_pages // npp),
        in_specs=[pl.BlockSpec((n_heads, 1), lambda b, p, pt: (0, 0)),
                  pl.BlockSpec((None, n_heads, hd), lambda b, p, pt: (b, 0, 0))]
                 + page_specs + page_specs
                 + [pl.BlockSpec(scan_mat.shape, lambda b, p, pt: (0, 0))],
        out_specs=pl.BlockSpec((None, n_heads, hd), lambda b, p, pt: (b, 0, 0)),
        scratch_shapes=[pltpu.VMEM((n_heads // 2, n_heads, 2 * hd), BF16),
                        pltpu.VMEM((n_heads, 2 * hd), F32),
                        pltpu.VMEM((n_heads, LANES), F32)],
    )
    return pl.pallas_call(
        functools.partial(_decode_kernel, qscale=qscale),
        grid_spec=grid_spec,
        out_shape=jax.ShapeDtypeStruct((bsz, n_heads, hd), BF16),
        compiler_params=_params("arbitrary", "arbitrary"),
        name="attn_decode",
    )(page_table, nbias2, q, *([cache_k] * len(page_specs)), *([cache_v] * len(page_specs)), scan_mat)


def _group_map(pooled, wg_ref, sc_ref, o_ref):
    gd = wg_ref.shape[1]
    for g in range(len(POOL_WINDOWS)):
        lanes = slice(g * gd, (g + 1) * gd)
        h = jnp.dot(pooled[g].astype(BF16), wg_ref[g], preferred_element_type=F32)
        o_ref[:, lanes] = (h * sc_ref[:, lanes]).astype(o_ref.dtype)


def _pool_prompt_kernel(u_ref, prev_ref, wg_ref, sc_ref, o_ref, *, halo):
    i = pl.program_id(0)
    tm = u_ref.shape[0]
    gd = wg_ref.shape[1]
    prev = jnp.where(i > 0, prev_ref[...], 0.0)
    pos1 = i * tm + lax.broadcasted_iota(jnp.int32, (tm, 1), 0) + 1
    pooled = []
    for g, win in enumerate(POOL_WINDOWS):
        lanes = slice(g * gd, (g + 1) * gd)
        u = u_ref[:, lanes]
        ext = jnp.concatenate([prev[:, lanes], u], axis=0)
        span = 1
        while span < win:
            ext = ext + pltpu.roll(ext, span, 0)
            span *= 2
        cnt = jnp.minimum(pos1, win).astype(F32)
        pooled.append(ext[halo:] / cnt - u)
    _group_map(pooled, wg_ref, sc_ref, o_ref)


def _pool_prompt(u, wg, sc, *, tm=256):
    s, d = u.shape
    halo = 16
    gd = wg.shape[1]
    return pl.pallas_call(
        functools.partial(_pool_prompt_kernel, halo=halo),
        grid=(s // tm,),
        in_specs=[pl.BlockSpec((tm, d), lambda i: (i, 0)),
                  pl.BlockSpec((halo, d), lambda i: (jnp.maximum(i * (tm // halo) - 1, 0), 0)),
                  pl.BlockSpec((len(POOL_WINDOWS), gd, gd), lambda i: (0, 0, 0)),
                  pl.BlockSpec((1, d), lambda i: (0, 0))],
        out_specs=pl.BlockSpec((tm, d), lambda i: (i, 0)),
        out_shape=jax.ShapeDtypeStruct((s, d), BF16),
        compiler_params=_params("arbitrary"),
        name="pool_prompt",
    )(u, u, wg, sc)


def _pool_sample_kernel(u_ref, buf_ref, wg_ref, sc_ref, o_ref, *, pos1):
    gd = wg_ref.shape[1]
    d = u_ref.shape[1]
    nbuf = buf_ref.shape[1] // d
    pooled = []
    for g, win in enumerate(POOL_WINDOWS):
        lanes = slice(g * gd, (g + 1) * gd)
        u = u_ref[:, lanes]
        total = u
        for r in range(nbuf - (win - 1), nbuf):
            total = total + buf_ref[:, r * d + g * gd:r * d + (g + 1) * gd]
        pooled.append(total / float(min(pos1, win)) - u)
    _group_map(pooled, wg_ref, sc_ref, o_ref)


def _pool_sample(u, buf, wg, sc, *, pos1, tb=32):
    bsz, d = u.shape
    nbuf = buf.shape[1]
    gd = wg.shape[1]
    buf = buf.reshape(bsz, nbuf * d)
    return pl.pallas_call(
        functools.partial(_pool_sample_kernel, pos1=pos1),
        grid=(bsz // tb,),
        in_specs=[pl.BlockSpec((tb, d), lambda i: (i, 0)),
                  pl.BlockSpec((tb, nbuf * d), lambda i: (i, 0)),
                  pl.BlockSpec((len(POOL_WINDOWS), gd, gd), lambda i: (0, 0, 0)),
                  pl.BlockSpec((1, d), lambda i: (0, 0))],
        out_specs=pl.BlockSpec((tb, d), lambda i: (i, 0)),
        out_shape=jax.ShapeDtypeStruct((bsz, d), BF16),
        compiler_params=_params("arbitrary"),
        name="pool_sample",
    )(u, buf, wg, sc)


def kernel(x_prompt, x_sample, cache_k, cache_v, state_pool, page_table, ln_g, ln_b, w_qkv, attn_bias,
           w_attn_out, w_pool_in, w_pool_group, pool_scale, w_pool_out, w_mlp_up, w_mlp_down):
    batch, seq, d = x_prompt.shape
    dec_b, dec_t, _ = x_sample.shape
    depth = ln_g.shape[0]
    n_heads = d // HEAD_DIM
    page = cache_k.shape[2]
    past_len = page_table.shape[1] * page
    pool_buf = state_pool.shape[2]
    assert batch == 1 and dec_t == 1 and pool_buf == max(POOL_WINDOWS) - 1
    alpha = (2.0 * depth) ** 0.25
    attn_scale = HEAD_DIM ** -0.5

    yp = x_prompt.reshape(seq, d)
    ys = x_sample.reshape(dec_b, d)

    kp_l, vp_l, ks_l, vs_l, pp_l, ps_l = [], [], [], [], [], []
    for layer in range(depth):
        g0, b0 = ln_g[layer, 0].reshape(1, d), ln_b[layer, 0].reshape(1, d)
        g1, b1 = ln_g[layer, 1].reshape(1, d), ln_b[layer, 1].reshape(1, d)
        if layer % 2 == 0:
            a = layer // 2
            wq = w_qkv[a, :, :d].astype(BF16)
            wk = w_qkv[a, :, d:2 * d].astype(BF16)
            wv = w_qkv[a, :, 2 * d:].astype(BF16)
            wo = w_attn_out[a].astype(BF16)
            qscale = -attn_scale * LOG2E
            nb2 = -LOG2E * attn_bias[a]
            xb = yp.astype(BF16)
            (qb,) = _matmul(xb, wq, (BF16,), scale=qscale)
            kf, kb = _kv_proj(xb, wk)
            vf, vb = _kv_proj(xb, wv)
            q_t = qb.reshape(seq, n_heads, HEAD_DIM).transpose(1, 2, 0)
            v_t = vb.reshape(seq // ATTN_TK, ATTN_TK, n_heads, HEAD_DIM).transpose(2, 0, 3, 1)
            mp = _attention_prompt(q_t, nb2, kb, v_t)
            kp_l.append(kf.reshape(batch, seq, n_heads, HEAD_DIM))
            vp_l.append(vf.reshape(batch, seq, n_heads, HEAD_DIM))
            (qs,) = _matmul(ys, wq, (F32,))
            ksn, _ = _kv_proj(ys, wk)
            vsn, _ = _kv_proj(ys, wv)
            ms = _attention_decode(qs.reshape(dec_b, n_heads, HEAD_DIM), cache_k, cache_v, page_table,
                                   nb2.reshape(n_heads, 1), layer=a, qscale=qscale).reshape(dec_b, d)
            ks_l.append(ksn.reshape(dec_b, dec_t, n_heads, HEAD_DIM))
            vs_l.append(vsn.reshape(dec_b, dec_t, n_heads, HEAD_DIM))
        else:
            p = layer // 2
            wi = w_pool_in[p].astype(BF16)
            wg = w_pool_group[p].astype(BF16)
            wo = w_pool_out[p].astype(BF16)
            sc = pool_scale[p].reshape(1, d)
            (up,) = _matmul(yp, wi, (F32,))
            mp = _pool_prompt(up, wg, sc)
            pp_l.append(up[seq - pool_buf:].reshape(batch, pool_buf, d))
            (us,) = _matmul(ys, wi, (F32,))
            ms = _pool_sample(us, state_pool[p], wg, sc, pos1=past_len + 1)
            ps_l.append(jnp.concatenate([state_pool[p][:, 1:], us[:, None, :]], axis=1))
        yp = _proj_ln(mp, wo, yp, g0, b0, alpha=alpha)
        ys = _proj_ln(ms, wo, ys, g0, b0, alpha=alpha)
        ys, wu, wd = _mlp_cast(ys, w_mlp_up, w_mlp_down, layer, g1, b1, alpha=alpha)
        yp = _mlp(yp, wu, wd, g1, b1, alpha=alpha)

    return (yp.reshape(batch, seq, d), ys.reshape(dec_b, dec_t, d),
            jnp.stack(kp_l), jnp.stack(vp_l), jnp.stack(pp_l),
            jnp.stack(ks_l), jnp.stack(vs_l), jnp.stack(ps_l))
```

```python
import functools
import math

import jax
import jax.numpy as jnp
from jax import lax
from jax.experimental import pallas as pl
from jax.experimental.pallas import tpu as pltpu

F32 = jnp.float32
BF16 = jnp.bfloat16

HEAD_DIM = 128
POOL_WINDOWS = (2, 4, 8, 16)
LN_EPS = 1e-5

LANES = 128
SUBLANES = 8
VMEM_LIMIT_BYTES = 56 * 1024 * 1024

ATTN_TQ = 512
ATTN_SUB = 128
ATTN_TK = ATTN_TQ
ATTN_HEADS_PER_STEP = 2
ATTN_TOTAL_ROWS = 16
DEC_PAGES_PER_STEP = 4
LOG2E = math.log2(math.e)
MASKED_LOG2 = -1e30
ATTN_DEAD_LOG2 = -160.0


def _params(*semantics):
    return pltpu.CompilerParams(dimension_semantics=semantics, vmem_limit_bytes=VMEM_LIMIT_BYTES)


def _layer_norm(xf, g, b):
    mu = jnp.mean(xf, axis=-1, keepdims=True)
    xc = xf - mu
    var = jnp.mean(xc * xc, axis=-1, keepdims=True)
    return xc * lax.rsqrt(var + LN_EPS) * g + b


def _matmul_kernel(a_ref, b_ref, *o_refs, scale):
    acc = jnp.dot(a_ref[...].astype(BF16), b_ref[...], preferred_element_type=F32)
    for o_ref in o_refs:
        if o_ref.dtype == BF16 and scale is not None:
            o_ref[...] = (acc * scale).astype(BF16)
        else:
            o_ref[...] = acc.astype(o_ref.dtype)


def _matmul(a, b, out_dtypes, *, scale=None, tm=1024, tn=1024):
    m, k = a.shape
    n = b.shape[1]
    tm, tn = min(tm, m), min(tn, n)
    outs = pl.pallas_call(
        functools.partial(_matmul_kernel, scale=scale),
        grid=(m // tm, n // tn),
        in_specs=[pl.BlockSpec((tm, k), lambda i, j: (i, 0)),
                  pl.BlockSpec((k, tn), lambda i, j: (0, j))],
        out_specs=[pl.BlockSpec((tm, tn), lambda i, j: (i, j)) for _ in out_dtypes],
        out_shape=[jax.ShapeDtypeStruct((m, n), dt) for dt in out_dtypes],
        compiler_params=_params("arbitrary", "arbitrary"),
        name="matmul",
    )(a, b)
    return outs


def _store_head_rows(o_ref, h, val):
    rows, tile_heads, hd = o_ref.shape
    o_ref.reshape(rows * tile_heads, hd)[pl.ds(h, rows, stride=tile_heads), :] = val


def _kv_kernel(x_ref, w_ref, f_ref, b_ref):
    acc = jnp.dot(x_ref[...].astype(BF16), w_ref[...], preferred_element_type=F32)
    b_ref[...] = acc.astype(BF16)
    hd = f_ref.shape[-1]
    for h in range(f_ref.shape[1]):
        _store_head_rows(f_ref, h, acc[:, h * hd:(h + 1) * hd])


def _kv_proj(x, w, *, tm=1024):
    m, d = x.shape
    n = w.shape[1]
    n_heads = n // HEAD_DIM
    tile_heads = SUBLANES if n_heads % SUBLANES == 0 else n_heads
    tn = tile_heads * HEAD_DIM
    tm = min(tm, m)
    return pl.pallas_call(
        _kv_kernel,
        grid=(m // tm, n // tn),
        in_specs=[pl.BlockSpec((tm, d), lambda i, j: (i, 0)),
                  pl.BlockSpec((d, tn), lambda i, j: (0, j))],
        out_specs=[pl.BlockSpec((tm, tile_heads, HEAD_DIM), lambda i, j: (i, j, 0)),
                   pl.BlockSpec((tm, tn), lambda i, j: (i, j))],
        out_shape=[jax.ShapeDtypeStruct((m, n_heads, HEAD_DIM), F32), jax.ShapeDtypeStruct((m, n), BF16)],
        compiler_params=_params("arbitrary", "arbitrary"),
        name="kv_proj",
    )(x, w)


def _proj_ln_kernel(a_ref, w_ref, x_ref, g_ref, b_ref, o_ref, *, alpha):
    m = jnp.dot(a_ref[...].astype(BF16), w_ref[...], preferred_element_type=F32)
    o_ref[...] = _layer_norm(alpha * x_ref[...] + m, g_ref[...], b_ref[...])


def _proj_ln(a, w, x, g, b, *, alpha, tm=512):
    m, k = a.shape
    d = w.shape[1]
    tm = min(tm, m)
    return pl.pallas_call(
        functools.partial(_proj_ln_kernel, alpha=alpha),
        grid=(m // tm,),
        in_specs=[pl.BlockSpec((tm, k), lambda i: (i, 0)),
                  pl.BlockSpec((k, d), lambda i: (0, 0)),
                  pl.BlockSpec((tm, d), lambda i: (i, 0)),
                  pl.BlockSpec((1, d), lambda i: (0, 0)),
                  pl.BlockSpec((1, d), lambda i: (0, 0))],
        out_specs=pl.BlockSpec((tm, d), lambda i: (i, 0)),
        out_shape=jax.ShapeDtypeStruct((m, d), F32),
        compiler_params=_params("arbitrary"),
        name="proj_ln",
    )(a, w, x, g, b)


def _mlp_init(kf, x_ref, acc_ref, xb_ref):
    @pl.when(kf == 0)
    def _():
        xb_ref[...] = x_ref[...].astype(BF16)
        acc_ref[...] = jnp.zeros_like(acc_ref)


def _mlp_accumulate(xb_ref, wu, wd, acc_ref):
    h = jnp.dot(xb_ref[...], wu, preferred_element_type=F32)
    h = jnp.maximum(h, 0.0)
    acc_ref[...] += jnp.dot((h * h).astype(BF16), wd, preferred_element_type=F32)


def _mlp_finish(kf, n_kf, x_ref, g_ref, b_ref, o_ref, acc_ref, alpha):
    @pl.when(kf == n_kf - 1)
    def _():
        o_ref[...] = _layer_norm(alpha * x_ref[...] + acc_ref[...], g_ref[...], b_ref[...])


def _mlp_kernel(x_ref, wu_ref, wd_ref, g_ref, b_ref, o_ref, acc_ref, xb_ref, *, alpha):
    kf = pl.program_id(1)
    _mlp_init(kf, x_ref, acc_ref, xb_ref)
    _mlp_accumulate(xb_ref, wu_ref[...], wd_ref[...], acc_ref)
    _mlp_finish(kf, pl.num_programs(1), x_ref, g_ref, b_ref, o_ref, acc_ref, alpha)


def _mlp_cast_kernel(x_ref, wu_ref, wd_ref, g_ref, b_ref, o_ref, wub_ref, wdb_ref, acc_ref, xb_ref, *, alpha):
    kf = pl.program_id(0)
    _mlp_init(kf, x_ref, acc_ref, xb_ref)
    wu = wu_ref[...].astype(BF16)
    wd = wd_ref[...].astype(BF16)
    wub_ref[...] = wu
    wdb_ref[...] = wd
    _mlp_accumulate(xb_ref, wu, wd, acc_ref)
    _mlp_finish(kf, pl.num_programs(0), x_ref, g_ref, b_ref, o_ref, acc_ref, alpha)


def _mlp_cast(x, w_up, w_down, layer, g, b, *, alpha, tf=512):
    m, d = x.shape
    f = w_up.shape[2]
    return pl.pallas_call(
        functools.partial(_mlp_cast_kernel, alpha=alpha),
        grid=(f // tf,),
        in_specs=[pl.BlockSpec((m, d), lambda kf: (0, 0)),
                  pl.BlockSpec((None, d, tf), lambda kf: (layer, 0, kf)),
                  pl.BlockSpec((None, tf, d), lambda kf: (layer, kf, 0)),
                  pl.BlockSpec((1, d), lambda kf: (0, 0)),
                  pl.BlockSpec((1, d), lambda kf: (0, 0))],
        out_specs=[pl.BlockSpec((m, d), lambda kf: (0, 0)),
                   pl.BlockSpec((d, tf), lambda kf: (0, kf)),
                   pl.BlockSpec((tf, d), lambda kf: (kf, 0))],
        out_shape=[jax.ShapeDtypeStruct((m, d), F32), jax.ShapeDtypeStruct((d, f), BF16),
                   jax.ShapeDtypeStruct((f, d), BF16)],
        scratch_shapes=[pltpu.VMEM((m, d), F32), pltpu.VMEM((m, d), BF16)],
        compiler_params=_params("arbitrary"),
        name="mlp_cast",
    )(x, w_up, w_down, g, b)


def _mlp(x, wu, wd, g, b, *, alpha, tm=512, tf=1024):
    m, d = x.shape
    f = wu.shape[1]
    tm = min(tm, m)
    return pl.pallas_call(
        functools.partial(_mlp_kernel, alpha=alpha),
        grid=(m // tm, f // tf),
        in_specs=[pl.BlockSpec((tm, d), lambda i, kf: (i, 0)),
                  pl.BlockSpec((d, tf), lambda i, kf: (0, kf)),
                  pl.BlockSpec((tf, d), lambda i, kf: (kf, 0)),
                  pl.BlockSpec((1, d), lambda i, kf: (0, 0)),
                  pl.BlockSpec((1, d), lambda i, kf: (0, 0))],
        out_specs=pl.BlockSpec((tm, d), lambda i, kf: (i, 0)),
        out_shape=jax.ShapeDtypeStruct((m, d), F32),
        scratch_shapes=[pltpu.VMEM((tm, d), F32), pltpu.VMEM((tm, d), BF16)],
        compiler_params=_params("arbitrary", "arbitrary"),
        name="mlp",
    )(x, wu, wd, g, b)


def _log2_stay_and_beta(w):
    ls = jnp.minimum(w, 0.0) - jnp.log(1.0 + jnp.exp2(-jnp.abs(w))) * LOG2E
    return ls, ls - w


def _split_bf16(x):
    hi = x.astype(BF16)
    lo = (x - hi.astype(F32)).astype(BF16)
    return hi, lo


def _pad_left(x, width):
    missing = width - x.shape[1]
    return x if missing == 0 else jnp.concatenate([jnp.zeros((x.shape[0], missing), x.dtype), x], axis=1)


def _attn_block(k_ref, vt_ref, k_one, scan_mat, q_aug, j, state, diagonal):
    start = pl.multiple_of(j * ATTN_TK, ATTN_TK)
    heads = range(ATTN_HEADS_PER_STEP)
    subs = list(reversed(range(ATTN_TK // ATTN_SUB)))
    first_col = {sub: sub * ATTN_SUB if diagonal else 0 for sub in subs}
    scores = {}
    for hh in heads:
        for sub in subs:
            k_sub = k_ref[pl.ds(start + sub * ATTN_SUB, ATTN_SUB), hh * HEAD_DIM:(hh + 1) * HEAD_DIM]
            scores[hh, sub] = jnp.dot(jnp.concatenate([k_sub, k_one], axis=1), q_aug[hh][:, first_col[sub]:],
                                      preferred_element_type=F32)
    log_beta, log_stay = {}, {}
    for hh in heads:
        for sub in subs:
            ls, lb = _log2_stay_and_beta(scores[hh, sub])
            if diagonal:
                row = lax.broadcasted_iota(jnp.int32, ls.shape, 0)
                col = lax.broadcasted_iota(jnp.int32, ls.shape, 1)
                mask = row < col
                ls = jnp.where(mask, ls, 0.0)
                lb = jnp.where(mask, lb, MASKED_LOG2)
            log_beta[hh, sub] = lb
            log_stay[hh, sub] = ls.astype(BF16)
    scans = {key: jnp.dot(scan_mat, ls, preferred_element_type=F32) for key, ls in log_stay.items()}
    new_state = []
    for hh in heads:
        acc, carry = state[hh]
        weights = []
        for sub in subs:
            res = scans[hh, sub]
            a = jnp.exp2(log_beta[hh, sub] + res[:ATTN_SUB] + carry[:, first_col[sub]:])
            weights.append(_pad_left(a.astype(BF16), ATTN_TQ))
            carry = carry + _pad_left(res[ATTN_SUB:ATTN_SUB + 1], ATTN_TQ)
        a = jnp.concatenate(weights[::-1], axis=0)
        new_state.append((acc + jnp.dot(vt_ref[hh, j], a, preferred_element_type=F32), carry))
    return tuple(new_state)


def _attn_kernel(qt_ref, qbias_ref, k_ref, vt_ref, k_one_ref, scan_ref, o_ref):
    i = pl.program_id(1)
    scan_mat = scan_ref[...]
    k_one = k_one_ref[...]
    heads = range(ATTN_HEADS_PER_STEP)
    q_aug = [jnp.concatenate([qt_ref[hh], qbias_ref[hh]], axis=0) for hh in heads]

    def block(j, state, diagonal):
        return _attn_block(k_ref, vt_ref, k_one, scan_mat, q_aug, j, state, diagonal)

    def body(carried):
        t, _, state = carried
        live = jnp.max(jnp.concatenate([state[hh][1] for hh in heads], axis=0)) > ATTN_DEAD_LOG2
        return t + 1, live.astype(jnp.int32), block(i - 1 - t, state, False)

    zero = (jnp.zeros((HEAD_DIM, ATTN_TQ), F32), jnp.zeros((1, ATTN_TQ), F32))
    state = block(i, tuple(zero for _ in heads), True)
    _, _, state = lax.while_loop(lambda c: (c[0] < i) & (c[1] != 0), body, (jnp.int32(0), jnp.int32(1), state))
    for hh in heads:
        o_ref[:, hh * HEAD_DIM:(hh + 1) * HEAD_DIM] = state[hh][0].T.astype(o_ref.dtype)


def _attention_prompt(q_t, nbias2, k_b, v_t):
    n_heads, hd, s = q_t.shape
    hps = ATTN_HEADS_PER_STEP
    nb_hi = nbias2.astype(BF16)
    nb_lo = (nbias2 - nb_hi.astype(F32)).astype(BF16)
    q_bias = jnp.zeros((n_heads, hd, ATTN_TQ), BF16)
    q_bias = q_bias.at[:, 0, :].set(nb_hi[:, None]).at[:, 1, :].set(nb_lo[:, None])
    k_one = jnp.zeros((ATTN_SUB, hd), BF16).at[:, :2].set(1.0)
    tri = (jnp.arange(ATTN_SUB)[None, :] > jnp.arange(ATTN_SUB)[:, None]).astype(BF16)
    scan_mat = jnp.concatenate([tri, jnp.ones((ATTN_TOTAL_ROWS, ATTN_SUB), BF16)], axis=0)
    return pl.pallas_call(
        _attn_kernel,
        grid=(n_heads // hps, s // ATTN_TQ),
        in_specs=[pl.BlockSpec((hps, hd, ATTN_TQ), lambda h, i: (h, 0, i)),
                  pl.BlockSpec((hps, hd, ATTN_TQ), lambda h, i: (h, 0, 0)),
                  pl.BlockSpec((s, hps * hd), lambda h, i: (0, h)),
                  pl.BlockSpec((hps, s // ATTN_TK, hd, ATTN_TK), lambda h, i: (h, 0, 0, 0)),
                  pl.BlockSpec(k_one.shape, lambda h, i: (0, 0)),
                  pl.BlockSpec(scan_mat.shape, lambda h, i: (0, 0))],
        out_specs=pl.BlockSpec((ATTN_TQ, hps * hd), lambda h, i: (i, h)),
        out_shape=jax.ShapeDtypeStruct((s, n_heads * hd), BF16),
        compiler_params=_params("arbitrary", "arbitrary"),
        name="attn_prompt",
    )(q_t, q_bias, k_b, v_t, k_one, scan_mat)


def _head_rows(block_ref, h):
    page, tile_heads, hd = block_ref.shape
    return block_ref.reshape(page * tile_heads, hd)[pl.ds(h, page, stride=tile_heads), :]


def _decode_kernel(pt_ref, nb_ref, q_ref, *refs, qscale):
    npp = DEC_PAGES_PER_STEP
    n_heads, hd = q_ref.shape
    page, tile_heads, _ = refs[0].shape
    n_tiles = n_heads // tile_heads
    n_blocks = npp * n_tiles
    k_refs, v_refs = refs[:n_blocks], refs[n_blocks:2 * n_blocks]
    scan_ref, o_ref, q2_ref, acc_ref, carry_ref = refs[2 * n_blocks:]
    p = pl.program_id(1)
    n_pairs = n_heads // 2
    row = lax.broadcasted_iota(jnp.int32, (n_heads, 2 * hd), 0)
    col = lax.broadcasted_iota(jnp.int32, (n_heads, 2 * hd), 1)
    own_half = (row % 2) == (col // hd)
    row_tok = lax.broadcasted_iota(jnp.int32, (n_heads, 2 * page), 0)

    @pl.when(p == 0)
    def _():
        qs = q_ref[...] * qscale
        qq = jnp.concatenate([qs, qs], axis=1)
        for pr in range(n_pairs):
            q2_ref[pr] = jnp.where((row // 2 == pr) & own_half, qq, 0.0).astype(BF16)
        acc_ref[...] = jnp.zeros_like(acc_ref)
        carry_ref[...] = jnp.zeros_like(carry_ref)

    def head_rows(page_refs, pg, h):
        return _head_rows(page_refs[pg * n_tiles + h // tile_heads], h % tile_heads)

    def pair_tile(page_refs, older, newer, pr):
        rows = [jnp.concatenate([head_rows(page_refs, pg, 2 * pr), head_rows(page_refs, pg, 2 * pr + 1)], axis=1)
                for pg in (older, newer)]
        return jnp.concatenate(rows, axis=0).astype(BF16)

    nb = nb_ref[...]
    acc = acc_ref[...]
    carry = carry_ref[...]
    page_pairs = [(c + 1, c) for c in range(0, npp, 2)]
    scores = []
    for older, newer in page_pairs:
        z = None
        for pr in range(n_pairs):
            zp = lax.dot_general(q2_ref[pr], pair_tile(k_refs, older, newer, pr), (((1,), (1,)), ((), ())),
                                 preferred_element_type=F32)
            z = zp if z is None else z + zp
        scores.append(z)
    log_beta, scans = [], []
    for z in scores:
        ls, lb = _log2_stay_and_beta(z + nb)
        hi, lo = _split_bf16(ls)
        log_beta.append(lb)
        scans.append(jnp.dot(jnp.concatenate([hi, lo], axis=1), scan_ref[...], preferred_element_type=F32))
    weights = []
    for lb, scan in zip(log_beta, scans):
        weights.append(jnp.exp2(lb + scan[:, :2 * page] + jnp.concatenate([carry] * (2 * page // LANES), axis=1)))
        carry = carry + scan[:, 2 * page:]
    for (older, newer), a in zip(page_pairs, weights):
        for pr in range(n_pairs):
            a_pr = jnp.where(row_tok // 2 == pr, a, 0.0).astype(BF16)
            acc = acc + jnp.dot(a_pr, pair_tile(v_refs, older, newer, pr), preferred_element_type=F32)
    acc_ref[...] = acc
    carry_ref[...] = carry

    @pl.when(p == pl.num_programs(1) - 1)
    def _():
        o_ref[...] = jnp.where(row[:, :hd] % 2 == 0, acc[:, :hd], acc[:, hd:]).astype(o_ref.dtype)


def _attention_decode(q, cache_k, cache_v, page_table, nbias2, *, layer, qscale):
    bsz, n_heads, hd = q.shape
    page = cache_k.shape[2]
    n_pages = page_table.shape[1]
    npp = DEC_PAGES_PER_STEP
    assert npp % 2 == 0 and n_pages % npp == 0 and n_heads % 2 == 0
    tri = (jnp.arange(2 * page)[:, None] > jnp.arange(2 * page)[None, :]).astype(BF16)
    scan_mat = jnp.tile(jnp.concatenate([tri, jnp.ones((2 * page, LANES), BF16)], axis=1), (2, 1))

    tile_heads = SUBLANES if n_heads % SUBLANES == 0 else n_heads
    n_tiles = n_heads // tile_heads

    def page_spec(c, t):
        return pl.BlockSpec((None, None, page, tile_heads, hd),
                            lambda b, p, pt: (layer, pt[b, n_pages - 1 - (p * npp + c)], 0, t, 0))

    page_specs = [page_spec(c, t) for c in range(npp) for t in range(n_tiles)]
    grid_spec = pltpu.PrefetchScalarGridSpec(
        num_scalar_prefetch=1,
        grid=(bsz, n_pages // npp),
        in_specs=[pl.BlockSpec((n_heads, 1), lambda b, p, pt: (0, 0)),
                  pl.BlockSpec((None, n_heads, hd), lambda b, p, pt: (b, 0, 0))]
                 + page_specs + page_specs
                 + [pl.BlockSpec(scan_mat.shape, lambda b, p, pt: (0, 0))],
        out_specs=pl.BlockSpec((None, n_heads, hd), lambda b, p, pt: (b, 0, 0)),
        scratch_shapes=[pltpu.VMEM((n_heads // 2, n_heads, 2 * hd), BF16),
                        pltpu.VMEM((n_heads, 2 * hd), F32),
                        pltpu.VMEM((n_heads, LANES), F32)],
    )
    return pl.pallas_call(
        functools.partial(_decode_kernel, qscale=qscale),
        grid_spec=grid_spec,
        out_shape=jax.ShapeDtypeStruct((bsz, n_heads, hd), BF16),
        compiler_params=_params("arbitrary", "arbitrary"),
        name="attn_decode",
    )(page_table, nbias2, q, *([cache_k] * len(page_specs)), *([cache_v] * len(page_specs)), scan_mat)


def _group_map(pooled, wg_ref, sc_ref, o_ref):
    gd = wg_ref.shape[1]
    for g in range(len(POOL_WINDOWS)):
        lanes = slice(g * gd, (g + 1) * gd)
        h = jnp.dot(pooled[g].astype(BF16), wg_ref[g], preferred_element_type=F32)
        o_ref[:, lanes] = (h * sc_ref[:, lanes]).astype(o_ref.dtype)


def _pool_prompt_kernel(u_ref, prev_ref, wg_ref, sc_ref, o_ref, *, halo):
    i = pl.program_id(0)
    tm = u_ref.shape[0]
    gd = wg_ref.shape[1]
    prev = jnp.where(i > 0, prev_ref[...], 0.0)
    pos1 = i * tm + lax.broadcasted_iota(jnp.int32, (tm, 1), 0) + 1
    pooled = []
    for g, win in enumerate(POOL_WINDOWS):
        lanes = slice(g * gd, (g + 1) * gd)
        u = u_ref[:, lanes]
        ext = jnp.concatenate([prev[:, lanes], u], axis=0)
        span = 1
        while span < win:
            ext = ext + pltpu.roll(ext, span, 0)
            span *= 2
        cnt = jnp.minimum(pos1, win).astype(F32)
        pooled.append(ext[halo:] / cnt - u)
    _group_map(pooled, wg_ref, sc_ref, o_ref)


def _pool_prompt(u, wg, sc, *, tm=256):
    s, d = u.shape
    halo = 16
    gd = wg.shape[1]
    return pl.pallas_call(
        functools.partial(_pool_prompt_kernel, halo=halo),
        grid=(s // tm,),
        in_specs=[pl.BlockSpec((tm, d), lambda i: (i, 0)),
                  pl.BlockSpec((halo, d), lambda i: (jnp.maximum(i * (tm // halo) - 1, 0), 0)),
                  pl.BlockSpec((len(POOL_WINDOWS), gd, gd), lambda i: (0, 0, 0)),
                  pl.BlockSpec((1, d), lambda i: (0, 0))],
        out_specs=pl.BlockSpec((tm, d), lambda i: (i, 0)),
        out_shape=jax.ShapeDtypeStruct((s, d), BF16),
        compiler_params=_params("arbitrary"),
        name="pool_prompt",
    )(u, u, wg, sc)


def _pool_sample_kernel(u_ref, buf_ref, wg_ref, sc_ref, o_ref, *, pos1):
    gd = wg_ref.shape[1]
    d = u_ref.shape[1]
    nbuf = buf_ref.shape[1] // d
    pooled = []
    for g, win in enumerate(POOL_WINDOWS):
        lanes = slice(g * gd, (g + 1) * gd)
        u = u_ref[:, lanes]
        total = u
        for r in range(nbuf - (win - 1), nbuf):
            total = total + buf_ref[:, r * d + g * gd:r * d + (g + 1) * gd]
        pooled.append(total / float(min(pos1, win)) - u)
    _group_map(pooled, wg_ref, sc_ref, o_ref)


def _pool_sample(u, buf, wg, sc, *, pos1, tb=32):
    bsz, d = u.shape
    nbuf = buf.shape[1]
    gd = wg.shape[1]
    buf = buf.reshape(bsz, nbuf * d)
    return pl.pallas_call(
        functools.partial(_pool_sample_kernel, pos1=pos1),
        grid=(bsz // tb,),
        in_specs=[pl.BlockSpec((tb, d), lambda i: (i, 0)),
                  pl.BlockSpec((tb, nbuf * d), lambda i: (i, 0)),
                  pl.BlockSpec((len(POOL_WINDOWS), gd, gd), lambda i: (0, 0, 0)),
                  pl.BlockSpec((1, d), lambda i: (0, 0))],
        out_specs=pl.BlockSpec((tb, d), lambda i: (i, 0)),
        out_shape=jax.ShapeDtypeStruct((bsz, d), BF16),
        compiler_params=_params("arbitrary"),
        name="pool_sample",
    )(u, buf, wg, sc)


def kernel(x_prompt, x_sample, cache_k, cache_v, state_pool, page_table, ln_g, ln_b, w_qkv, attn_bias,
           w_attn_out, w_pool_in, w_pool_group, pool_scale, w_pool_out, w_mlp_up, w_mlp_down):
    batch, seq, d = x_prompt.shape
    dec_b, dec_t, _ = x_sample.shape
    depth = ln_g.shape[0]
    n_heads = d // HEAD_DIM
    page = cache_k.shape[2]
    past_len = page_table.shape[1] * page
    pool_buf = state_pool.shape[2]
    assert batch == 1 and dec_t == 1 and pool_buf == max(POOL_WINDOWS) - 1
    alpha = (2.0 * depth) ** 0.25
    attn_scale = HEAD_DIM ** -0.5

    yp = x_prompt.reshape(seq, d)
    ys = x_sample.reshape(dec_b, d)

    kp_l, vp_l, ks_l, vs_l, pp_l, ps_l = [], [], [], [], [], []
    for layer in range(depth):
        g0, b0 = ln_g[layer, 0].reshape(1, d), ln_b[layer, 0].reshape(1, d)
        g1, b1 = ln_g[layer, 1].reshape(1, d), ln_b[layer, 1].reshape(1, d)
        if layer % 2 == 0:
            a = layer // 2
            wq = w_qkv[a, :, :d].astype(BF16)
            wk = w_qkv[a, :, d:2 * d].astype(BF16)
            wv = w_qkv[a, :, 2 * d:].astype(BF16)
            wo = w_attn_out[a].astype(BF16)
            qscale = -attn_scale * LOG2E
            nb2 = -LOG2E * attn_bias[a]
            xb = yp.astype(BF16)
            (qb,) = _matmul(xb, wq, (BF16,), scale=qscale)
            kf, kb = _kv_proj(xb, wk)
            vf, vb = _kv_proj(xb, wv)
            q_t = qb.reshape(seq, n_heads, HEAD_DIM).transpose(1, 2, 0)
            v_t = vb.reshape(seq // ATTN_TK, ATTN_TK, n_heads, HEAD_DIM).transpose(2, 0, 3, 1)
            mp = _attention_prompt(q_t, nb2, kb, v_t)
            kp_l.append(kf.reshape(batch, seq, n_heads, HEAD_DIM))
            vp_l.append(vf.reshape(batch, seq, n_heads, HEAD_DIM))
            (qs,) = _matmul(ys, wq, (F32,))
            ksn, _ = _kv_proj(ys, wk)
            vsn, _ = _kv_proj(ys, wv)
            ms = _attention_decode(qs.reshape(dec_b, n_heads, HEAD_DIM), cache_k, cache_v, page_table,
                                   nb2.reshape(n_heads, 1), layer=a, qscale=qscale).reshape(dec_b, d)
            ks_l.append(ksn.reshape(dec_b, dec_t, n_heads, HEAD_DIM))
            vs_l.append(vsn.reshape(dec_b, dec_t, n_heads, HEAD_DIM))
        else:
            p = layer // 2
            wi = w_pool_in[p].astype(BF16)
            wg = w_pool_group[p].astype(BF16)
            wo = w_pool_out[p].astype(BF16)
            sc = pool_scale[p].reshape(1, d)
            (up,) = _matmul(yp, wi, (F32,))
            mp = _pool_prompt(up, wg, sc)
            pp_l.append(up[seq - pool_buf:].reshape(batch, pool_buf, d))
            (us,) = _matmul(ys, wi, (F32,))
            ms = _pool_sample(us, state_pool[p], wg, sc, pos1=past_len + 1)
            ps_l.append(jnp.concatenate([state_pool[p][:, 1:], us[:, None, :]], axis=1))
        yp = _proj_ln(mp, wo, yp, g0, b0, alpha=alpha)
        ys = _proj_ln(ms, wo, ys, g0, b0, alpha=alpha)
        ys, wu, wd = _mlp_cast(ys, w_mlp_up, w_mlp_down, layer, g1, b1, alpha=alpha)
        yp = _mlp(yp, wu, wd, g1, b1, alpha=alpha)

    return (yp.reshape(batch, seq, d), ys.reshape(dec_b, dec_t, d),
            jnp.stack(kp_l), jnp.stack(vp_l), jnp.stack(pp_l),
            jnp.stack(ks_l), jnp.stack(vs_l), jnp.stack(ps_l))
```

```python
import functools
import math

import jax
import jax.numpy as jnp
from jax import lax
from jax.experimental import pallas as pl
from jax.experimental.pallas import tpu as pltpu

F32 = jnp.float32
BF16 = jnp.bfloat16

HEAD_DIM = 128
POOL_WINDOWS = (2, 4, 8, 16)
LN_EPS = 1e-5

LANES = 128
SUBLANES = 8
VMEM_LIMIT_BYTES = 56 * 1024 * 1024

ATTN_TQ = 512
ATTN_SUB = 256
ATTN_TK = ATTN_TQ
ATTN_HEADS_PER_STEP = 2
ATTN_TOTAL_ROWS = 16
DEC_PAGES_PER_STEP = 4
LOG2E = math.log2(math.e)
MASKED_LOG2 = -1e30
ATTN_DEAD_LOG2 = -160.0


def _params(*semantics):
    return pltpu.CompilerParams(dimension_semantics=semantics, vmem_limit_bytes=VMEM_LIMIT_BYTES)


def _layer_norm(xf, g, b):
    mu = jnp.mean(xf, axis=-1, keepdims=True)
    xc = xf - mu
    var = jnp.mean(xc * xc, axis=-1, keepdims=True)
    return xc * lax.rsqrt(var + LN_EPS) * g + b


def _matmul_kernel(a_ref, b_ref, o_ref, *, scale):
    acc = jnp.dot(a_ref[...].astype(BF16), b_ref[...], preferred_element_type=F32)
    o_ref[...] = (acc if scale is None else acc * scale).astype(o_ref.dtype)


def _matmul(a, b, out_dtype, *, scale=None, tm=512):
    m, k = a.shape
    n = b.shape[1]
    tm = min(tm, m)
    return pl.pallas_call(
        functools.partial(_matmul_kernel, scale=scale),
        grid=(m // tm,),
        in_specs=[pl.BlockSpec((tm, k), lambda i: (i, 0)),
                  pl.BlockSpec((k, n), lambda i: (0, 0))],
        out_specs=pl.BlockSpec((tm, n), lambda i: (i, 0)),
        out_shape=jax.ShapeDtypeStruct((m, n), out_dtype),
        compiler_params=_params("arbitrary"),
        name="matmul",
    )(a, b)


def _store_head_rows(o_ref, h, val):
    rows, tile_heads, hd = o_ref.shape
    o_ref.reshape(rows * tile_heads, hd)[pl.ds(h, rows, stride=tile_heads), :] = val


def _kv_kernel(x_ref, w_ref, f_ref, b_ref):
    acc = jnp.dot(x_ref[...].astype(BF16), w_ref[...], preferred_element_type=F32)
    b_ref[...] = acc.astype(BF16)
    hd = f_ref.shape[-1]
    for h in range(f_ref.shape[1]):
        _store_head_rows(f_ref, h, acc[:, h * hd:(h + 1) * hd])


def _kv_proj(x, w, *, tm=1024):
    m, d = x.shape
    n = w.shape[1]
    n_heads = n // HEAD_DIM
    tile_heads = SUBLANES if n_heads % SUBLANES == 0 else n_heads
    tn = tile_heads * HEAD_DIM
    tm = min(tm, m)
    return pl.pallas_call(
        _kv_kernel,
        grid=(m // tm, n // tn),
        in_specs=[pl.BlockSpec((tm, d), lambda i, j: (i, 0)),
                  pl.BlockSpec((d, tn), lambda i, j: (0, j))],
        out_specs=[pl.BlockSpec((tm, tile_heads, HEAD_DIM), lambda i, j: (i, j, 0)),
                   pl.BlockSpec((tm, tn), lambda i, j: (i, j))],
        out_shape=[jax.ShapeDtypeStruct((m, n_heads, HEAD_DIM), F32), jax.ShapeDtypeStruct((m, n), BF16)],
        compiler_params=_params("arbitrary", "arbitrary"),
        name="kv_proj",
    )(x, w)


def _proj_ln_kernel(a_ref, w_ref, x_ref, g_ref, b_ref, o_ref, *, alpha):
    m = jnp.dot(a_ref[...].astype(BF16), w_ref[...], preferred_element_type=F32)
    o_ref[...] = _layer_norm(alpha * x_ref[...] + m, g_ref[...], b_ref[...])


def _proj_ln(a, w, x, g, b, *, alpha, tm=512):
    m, k = a.shape
    d = w.shape[1]
    tm = min(tm, m)
    return pl.pallas_call(
        functools.partial(_proj_ln_kernel, alpha=alpha),
        grid=(m // tm,),
        in_specs=[pl.BlockSpec((tm, k), lambda i: (i, 0)),
                  pl.BlockSpec((k, d), lambda i: (0, 0)),
                  pl.BlockSpec((tm, d), lambda i: (i, 0)),
                  pl.BlockSpec((1, d), lambda i: (0, 0)),
                  pl.BlockSpec((1, d), lambda i: (0, 0))],
        out_specs=pl.BlockSpec((tm, d), lambda i: (i, 0)),
        out_shape=jax.ShapeDtypeStruct((m, d), F32),
        compiler_params=_params("arbitrary"),
        name="proj_ln",
    )(a, w, x, g, b)


def _mlp_init(kf, x_ref, acc_ref, xb_ref):
    @pl.when(kf == 0)
    def _():
        xb_ref[...] = x_ref[...].astype(BF16)
        acc_ref[...] = jnp.zeros_like(acc_ref)


def _mlp_accumulate(xb_ref, wu, wd, acc_ref):
    h = jnp.dot(xb_ref[...], wu, preferred_element_type=F32)
    h = jnp.maximum(h, 0.0)
    acc_ref[...] += jnp.dot((h * h).astype(BF16), wd, preferred_element_type=F32)


def _mlp_finish(kf, n_kf, x_ref, g_ref, b_ref, o_ref, acc_ref, alpha):
    @pl.when(kf == n_kf - 1)
    def _():
        o_ref[...] = _layer_norm(alpha * x_ref[...] + acc_ref[...], g_ref[...], b_ref[...])


def _mlp_kernel(x_ref, wu_ref, wd_ref, g_ref, b_ref, o_ref, acc_ref, xb_ref, *, alpha):
    kf = pl.program_id(1)
    _mlp_init(kf, x_ref, acc_ref, xb_ref)
    _mlp_accumulate(xb_ref, wu_ref[...], wd_ref[...], acc_ref)
    _mlp_finish(kf, pl.num_programs(1), x_ref, g_ref, b_ref, o_ref, acc_ref, alpha)


def _mlp_cast_kernel(x_ref, wu_ref, wd_ref, g_ref, b_ref, o_ref, wub_ref, wdb_ref, acc_ref, xb_ref, *, alpha):
    kf = pl.program_id(0)
    _mlp_init(kf, x_ref, acc_ref, xb_ref)
    wu = wu_ref[...].astype(BF16)
    wd = wd_ref[...].astype(BF16)
    wub_ref[...] = wu
    wdb_ref[...] = wd
    _mlp_accumulate(xb_ref, wu, wd, acc_ref)
    _mlp_finish(kf, pl.num_programs(0), x_ref, g_ref, b_ref, o_ref, acc_ref, alpha)


def _mlp_cast(x, w_up, w_down, layer, g, b, *, alpha, tf=512):
    m, d = x.shape
    f = w_up.shape[2]
    return pl.pallas_call(
        functools.partial(_mlp_cast_kernel, alpha=alpha),
        grid=(f // tf,),
        in_specs=[pl.BlockSpec((m, d), lambda kf: (0, 0)),
                  pl.BlockSpec((None, d, tf), lambda kf: (layer, 0, kf)),
                  pl.BlockSpec((None, tf, d), lambda kf: (layer, kf, 0)),
                  pl.BlockSpec((1, d), lambda kf: (0, 0)),
                  pl.BlockSpec((1, d), lambda kf: (0, 0))],
        out_specs=[pl.BlockSpec((m, d), lambda kf: (0, 0)),
                   pl.BlockSpec((d, tf), lambda kf: (0, kf)),
                   pl.BlockSpec((tf, d), lambda kf: (kf, 0))],
        out_shape=[jax.ShapeDtypeStruct((m, d), F32), jax.ShapeDtypeStruct((d, f), BF16),
                   jax.ShapeDtypeStruct((f, d), BF16)],
        scratch_shapes=[pltpu.VMEM((m, d), F32), pltpu.VMEM((m, d), BF16)],
        compiler_params=_params("arbitrary"),
        name="mlp_cast",
    )(x, w_up, w_down, g, b)


def _mlp(x, wu, wd, g, b, *, alpha, tm=512, tf=1024):
    m, d = x.shape
    f = wu.shape[1]
    tm = min(tm, m)
    return pl.pallas_call(
        functools.partial(_mlp_kernel, alpha=alpha),
        grid=(m // tm, f // tf),
        in_specs=[pl.BlockSpec((tm, d), lambda i, kf: (i, 0)),
                  pl.BlockSpec((d, tf), lambda i, kf: (0, kf)),
                  pl.BlockSpec((tf, d), lambda i, kf: (kf, 0)),
                  pl.BlockSpec((1, d), lambda i, kf: (0, 0)),
                  pl.BlockSpec((1, d), lambda i, kf: (0, 0))],
        out_specs=pl.BlockSpec((tm, d), lambda i, kf: (i, 0)),
        out_shape=jax.ShapeDtypeStruct((m, d), F32),
        scratch_shapes=[pltpu.VMEM((tm, d), F32), pltpu.VMEM((tm, d), BF16)],
        compiler_params=_params("arbitrary", "arbitrary"),
        name="mlp",
    )(x, wu, wd, g, b)


def _log2_stay_and_beta(w):
    ls = jnp.minimum(w, 0.0) - jnp.log(1.0 + jnp.exp2(-jnp.abs(w))) * LOG2E
    return ls, ls - w


def _split_bf16(x):
    hi = x.astype(BF16)
    lo = (x - hi.astype(F32)).astype(BF16)
    return hi, lo


def _pad_left(x, width):
    missing = width - x.shape[1]
    return x if missing == 0 else jnp.concatenate([jnp.zeros((x.shape[0], missing), x.dtype), x], axis=1)


def _attn_block(k_ref, vt_ref, k_one, scan_mat, q_aug, j, state, diagonal):
    start = pl.multiple_of(j * ATTN_TK, ATTN_TK)
    heads = range(ATTN_HEADS_PER_STEP)
    subs = list(reversed(range(ATTN_TK // ATTN_SUB)))
    first_col = {sub: sub * ATTN_SUB if diagonal else 0 for sub in subs}
    scores = {}
    for hh in heads:
        for sub in subs:
            k_sub = k_ref[pl.ds(start + sub * ATTN_SUB, ATTN_SUB), hh * HEAD_DIM:(hh + 1) * HEAD_DIM]
            scores[hh, sub] = jnp.dot(jnp.concatenate([k_sub, k_one], axis=1), q_aug[hh][:, first_col[sub]:],
                                      preferred_element_type=F32)
    log_beta, log_stay = {}, {}
    for hh in heads:
        for sub in subs:
            ls, lb = _log2_stay_and_beta(scores[hh, sub])
            if diagonal:
                row = lax.broadcasted_iota(jnp.int32, ls.shape, 0)
                col = lax.broadcasted_iota(jnp.int32, ls.shape, 1)
                mask = row < col
                ls = jnp.where(mask, ls, 0.0)
                lb = jnp.where(mask, lb, MASKED_LOG2)
            log_beta[hh, sub] = lb
            log_stay[hh, sub] = ls.astype(BF16)
    scans = {key: jnp.dot(scan_mat, ls, preferred_element_type=F32) for key, ls in log_stay.items()}
    new_state = []
    for hh in heads:
        acc, carry = state[hh]
        weights = []
        for sub in subs:
            res = scans[hh, sub]
            a = jnp.exp2(log_beta[hh, sub] + res[:ATTN_SUB] + carry[:, first_col[sub]:])
            weights.append(_pad_left(a.astype(BF16), ATTN_TQ))
            carry = carry + _pad_left(res[ATTN_SUB:ATTN_SUB + 1], ATTN_TQ)
        a = jnp.concatenate(weights[::-1], axis=0)
        new_state.append((acc + jnp.dot(vt_ref[hh, j], a, preferred_element_type=F32), carry))
    return tuple(new_state)


def _attn_kernel(qt_ref, qbias_ref, k_ref, vt_ref, k_one_ref, scan_ref, o_ref):
    i = pl.program_id(1)
    scan_mat = scan_ref[...]
    k_one = k_one_ref[...]
    heads = range(ATTN_HEADS_PER_STEP)
    q_aug = [jnp.concatenate([qt_ref[hh], qbias_ref[hh]], axis=0) for hh in heads]

    def block(j, state, diagonal):
        return _attn_block(k_ref, vt_ref, k_one, scan_mat, q_aug, j, state, diagonal)

    def body(carried):
        t, _, state = carried
        live = jnp.max(jnp.concatenate([state[hh][1] for hh in heads], axis=0)) > ATTN_DEAD_LOG2
        return t + 1, live.astype(jnp.int32), block(i - 1 - t, state, False)

    zero = (jnp.zeros((HEAD_DIM, ATTN_TQ), F32), jnp.zeros((1, ATTN_TQ), F32))
    state = block(i, tuple(zero for _ in heads), True)
    _, _, state = lax.while_loop(lambda c: (c[0] < i) & (c[1] != 0), body, (jnp.int32(0), jnp.int32(1), state))
    for hh in heads:
        o_ref[:, hh * HEAD_DIM:(hh + 1) * HEAD_DIM] = state[hh][0].T.astype(o_ref.dtype)


def _attention_prompt(q_t, nbias2, k_b, v_t):
    n_heads, hd, s = q_t.shape
    hps = ATTN_HEADS_PER_STEP
    nb_hi = nbias2.astype(BF16)
    nb_lo = (nbias2 - nb_hi.astype(F32)).astype(BF16)
    q_bias = jnp.zeros((n_heads, hd, ATTN_TQ), BF16)
    q_bias = q_bias.at[:, 0, :].set(nb_hi[:, None]).at[:, 1, :].set(nb_lo[:, None])
    k_one = jnp.zeros((ATTN_SUB, hd), BF16).at[:, :2].set(1.0)
    tri = (jnp.arange(ATTN_SUB)[None, :] > jnp.arange(ATTN_SUB)[:, None]).astype(BF16)
    scan_mat = jnp.concatenate([tri, jnp.ones((ATTN_TOTAL_ROWS, ATTN_SUB), BF16)], axis=0)
    return pl.pallas_call(
        _attn_kernel,
        grid=(n_heads // hps, s // ATTN_TQ),
        in_specs=[pl.BlockSpec((hps, hd, ATTN_TQ), lambda h, i: (h, 0, i)),
                  pl.BlockSpec((hps, hd, ATTN_TQ), lambda h, i: (h, 0, 0)),
                  pl.BlockSpec((s, hps * hd), lambda h, i: (0, h)),
                  pl.BlockSpec((hps, s // ATTN_TK, hd, ATTN_TK), lambda h, i: (h, 0, 0, 0)),
                  pl.BlockSpec(k_one.shape, lambda h, i: (0, 0)),
                  pl.BlockSpec(scan_mat.shape, lambda h, i: (0, 0))],
        out_specs=pl.BlockSpec((ATTN_TQ, hps * hd), lambda h, i: (i, h)),
        out_shape=jax.ShapeDtypeStruct((s, n_heads * hd), BF16),
        compiler_params=_params("arbitrary", "arbitrary"),
        name="attn_prompt",
    )(q_t, q_bias, k_b, v_t, k_one, scan_mat)


def _head_rows(block_ref, h):
    page, tile_heads, hd = block_ref.shape
    return block_ref.reshape(page * tile_heads, hd)[pl.ds(h, page, stride=tile_heads), :]


def _decode_kernel(pt_ref, nb_ref, q_ref, *refs, qscale):
    npp = DEC_PAGES_PER_STEP
    n_heads, hd = q_ref.shape
    page, tile_heads, _ = refs[0].shape
    n_tiles = n_heads // tile_heads
    n_blocks = npp * n_tiles
    k_refs, v_refs = refs[:n_blocks], refs[n_blocks:2 * n_blocks]
    scan_ref, o_ref, q2_ref, acc_ref, carry_ref = refs[2 * n_blocks:]
    p = pl.program_id(1)
    n_pairs = n_heads // 2
    row = lax.broadcasted_iota(jnp.int32, (n_heads, 2 * hd), 0)
    col = lax.broadcasted_iota(jnp.int32, (n_heads, 2 * hd), 1)
    own_half = (row % 2) == (col // hd)
    row_tok = lax.broadcasted_iota(jnp.int32, (n_heads, 2 * page), 0)

    @pl.when(p == 0)
    def _():
        qs = q_ref[...] * qscale
        qq = jnp.concatenate([qs, qs], axis=1)
        for pr in range(n_pairs):
            q2_ref[pr] = jnp.where((row // 2 == pr) & own_half, qq, 0.0).astype(BF16)
        acc_ref[...] = jnp.zeros_like(acc_ref)
        carry_ref[...] = jnp.zeros_like(carry_ref)

    def head_rows(page_refs, pg, h):
        return _head_rows(page_refs[pg * n_tiles + h // tile_heads], h % tile_heads)

    def pair_tile(page_refs, older, newer, pr):
        rows = [jnp.concatenate([head_rows(page_refs, pg, 2 * pr), head_rows(page_refs, pg, 2 * pr + 1)], axis=1)
                for pg in (older, newer)]
        return jnp.concatenate(rows, axis=0).astype(BF16)

    nb = nb_ref[...]
    acc = acc_ref[...]
    carry = carry_ref[...]
    page_pairs = [(c + 1, c) for c in range(0, npp, 2)]
    scores = []
    for older, newer in page_pairs:
        z = None
        for pr in range(n_pairs):
            zp = lax.dot_general(q2_ref[pr], pair_tile(k_refs, older, newer, pr), (((1,), (1,)), ((), ())),
                                 preferred_element_type=F32)
            z = zp if z is None else z + zp
        scores.append(z)
    log_beta, scans = [], []
    for z in scores:
        ls, lb = _log2_stay_and_beta(z + nb)
        hi, lo = _split_bf16(ls)
        log_beta.append(lb)
        scans.append(jnp.dot(jnp.concatenate([hi, lo], axis=1), scan_ref[...], preferred_element_type=F32))
    weights = []
    for lb, scan in zip(log_beta, scans):
        weights.append(jnp.exp2(lb + scan[:, :2 * page] + jnp.concatenate([carry] * (2 * page // LANES), axis=1)))
        carry = carry + scan[:, 2 * page:]
    for (older, newer), a in zip(page_pairs, weights):
        for pr in range(n_pairs):
            a_pr = jnp.where(row_tok // 2 == pr, a, 0.0).astype(BF16)
            acc = acc + jnp.dot(a_pr, pair_tile(v_refs, older, newer, pr), preferred_element_type=F32)
    acc_ref[...] = acc
    carry_ref[...] = carry

    @pl.when(p == pl.num_programs(1) - 1)
    def _():
        o_ref[...] = jnp.where(row[:, :hd] % 2 == 0, acc[:, :hd], acc[:, hd:]).astype(o_ref.dtype)


def _attention_decode(q, cache_k, cache_v, page_table, nbias2, *, layer, qscale):
    bsz, n_heads, hd = q.shape
    page = cache_k.shape[2]
    n_pages = page_table.shape[1]
    npp = DEC_PAGES_PER_STEP
    assert npp % 2 == 0 and n_pages % npp == 0 and n_heads % 2 == 0
    tri = (jnp.arange(2 * page)[:, None] > jnp.arange(2 * page)[None, :]).astype(BF16)
    scan_mat = jnp.tile(jnp.concatenate([tri, jnp.ones((2 * page, LANES), BF16)], axis=1), (2, 1))

    tile_heads = SUBLANES if n_heads % SUBLANES == 0 else n_heads
    n_tiles = n_heads // tile_heads

    def page_spec(c, t):
        return pl.BlockSpec((None, None, page, tile_heads, hd),
                            lambda b, p, pt: (layer, pt[b, n_pages - 1 - (p * npp + c)], 0, t, 0))

    page_specs = [page_spec(c, t) for c in range(npp) for t in range(n_tiles)]
    grid_spec = pltpu.PrefetchScalarGridSpec(
        num_scalar_prefetch=1,
        grid=(bsz, n_pages // npp),
        in_specs=[pl.BlockSpec((n_heads, 1), lambda b, p, pt: (0, 0)),
                  pl.BlockSpec((None, n_heads, hd), lambda b, p, pt: (b, 0, 0))]
                 + page_specs + page_specs
                 + [pl.BlockSpec(scan_mat.shape, lambda b, p, pt: (0, 0))],
        out_specs=pl.BlockSpec((None, n_heads, hd), lambda b, p, pt: (b, 0, 0)),
        scratch_shapes=[pltpu.VMEM((n_heads // 2, n_heads, 2 * hd), BF16),
                        pltpu.VMEM((n_heads, 2 * hd), F32),
                        pltpu.VMEM((n_heads, LANES), F32)],
    )
    return pl.pallas_call(
        functools.partial(_decode_kernel, qscale=qscale),
        grid_spec=grid_spec,
        out_shape=jax.ShapeDtypeStruct((bsz, n_heads, hd), BF16),
        compiler_params=_params("arbitrary", "arbitrary"),
        name="attn_decode",
    )(page_table, nbias2, q, *([cache_k] * len(page_specs)), *([cache_v] * len(page_specs)), scan_mat)


def _group_map(pooled, wg_ref, sc_ref, o_ref):
    gd = wg_ref.shape[1]
    for g in range(len(POOL_WINDOWS)):
        lanes = slice(g * gd, (g + 1) * gd)
        h = jnp.dot(pooled[g].astype(BF16), wg_ref[g], preferred_element_type=F32)
        o_ref[:, lanes] = (h * sc_ref[:, lanes]).astype(o_ref.dtype)


def _pool_prompt_kernel(u_ref, prev_ref, wg_ref, sc_ref, o_ref, *, halo):
    i = pl.program_id(0)
    tm = u_ref.shape[0]
    gd = wg_ref.shape[1]
    prev = jnp.where(i > 0, prev_ref[...], 0.0)
    pos1 = i * tm + lax.broadcasted_iota(jnp.int32, (tm, 1), 0) + 1
    pooled = []
    for g, win in enumerate(POOL_WINDOWS):
        lanes = slice(g * gd, (g + 1) * gd)
        u = u_ref[:, lanes]
        ext = jnp.concatenate([prev[:, lanes], u], axis=0)
        span = 1
        while span < win:
            ext = ext + pltpu.roll(ext, span, 0)
            span *= 2
        cnt = jnp.minimum(pos1, win).astype(F32)
        pooled.append(ext[halo:] / cnt - u)
    _group_map(pooled, wg_ref, sc_ref, o_ref)


def _pool_prompt(u, wg, sc, *, tm=256):
    s, d = u.shape
    halo = 16
    gd = wg.shape[1]
    return pl.pallas_call(
        functools.partial(_pool_prompt_kernel, halo=halo),
        grid=(s // tm,),
        in_specs=[pl.BlockSpec((tm, d), lambda i: (i, 0)),
                  pl.BlockSpec((halo, d), lambda i: (jnp.maximum(i * (tm // halo) - 1, 0), 0)),
                  pl.BlockSpec((len(POOL_WINDOWS), gd, gd), lambda i: (0, 0, 0)),
                  pl.BlockSpec((1, d), lambda i: (0, 0))],
        out_specs=pl.BlockSpec((tm, d), lambda i: (i, 0)),
        out_shape=jax.ShapeDtypeStruct((s, d), BF16),
        compiler_params=_params("arbitrary"),
        name="pool_prompt",
    )(u, u, wg, sc)


def _pool_sample_kernel(u_ref, buf_ref, wg_ref, sc_ref, o_ref, *, pos1):
    gd = wg_ref.shape[1]
    d = u_ref.shape[1]
    nbuf = buf_ref.shape[1] // d
    pooled = []
    for g, win in enumerate(POOL_WINDOWS):
        lanes = slice(g * gd, (g + 1) * gd)
        u = u_ref[:, lanes]
        total = u
        for r in range(nbuf - (win - 1), nbuf):
            total = total + buf_ref[:, r * d + g * gd:r * d + (g + 1) * gd]
        pooled.append(total / float(min(pos1, win)) - u)
    _group_map(pooled, wg_ref, sc_ref, o_ref)


def _pool_sample(u, buf, wg, sc, *, pos1, tb=32):
    bsz, d = u.shape
    nbuf = buf.shape[1]
    gd = wg.shape[1]
    buf = buf.reshape(bsz, nbuf * d)
    return pl.pallas_call(
        functools.partial(_pool_sample_kernel, pos1=pos1),
        grid=(bsz // tb,),
        in_specs=[pl.BlockSpec((tb, d), lambda i: (i, 0)),
                  pl.BlockSpec((tb, nbuf * d), lambda i: (i, 0)),
                  pl.BlockSpec((len(POOL_WINDOWS), gd, gd), lambda i: (0, 0, 0)),
                  pl.BlockSpec((1, d), lambda i: (0, 0))],
        out_specs=pl.BlockSpec((tb, d), lambda i: (i, 0)),
        out_shape=jax.ShapeDtypeStruct((bsz, d), BF16),
        compiler_params=_params("arbitrary"),
        name="pool_sample",
    )(u, buf, wg, sc)


def kernel(x_prompt, x_sample, cache_k, cache_v, state_pool, page_table, ln_g, ln_b, w_qkv, attn_bias,
           w_attn_out, w_pool_in, w_pool_group, pool_scale, w_pool_out, w_mlp_up, w_mlp_down):
    batch, seq, d = x_prompt.shape
    dec_b, dec_t, _ = x_sample.shape
    depth = ln_g.shape[0]
    n_heads = d // HEAD_DIM
    page = cache_k.shape[2]
    past_len = page_table.shape[1] * page
    pool_buf = state_pool.shape[2]
    assert batch == 1 and dec_t == 1 and pool_buf == max(POOL_WINDOWS) - 1
    alpha = (2.0 * depth) ** 0.25
    attn_scale = HEAD_DIM ** -0.5

    yp = x_prompt.reshape(seq, d)
    ys = x_sample.reshape(dec_b, d)

    kp_l, vp_l, ks_l, vs_l, pp_l, ps_l = [], [], [], [], [], []
    for layer in range(depth):
        g0, b0 = ln_g[layer, 0].reshape(1, d), ln_b[layer, 0].reshape(1, d)
        g1, b1 = ln_g[layer, 1].reshape(1, d), ln_b[layer, 1].reshape(1, d)
        if layer % 2 == 0:
            a = layer // 2
            wq = w_qkv[a, :, :d].astype(BF16)
            wk = w_qkv[a, :, d:2 * d].astype(BF16)
            wv = w_qkv[a, :, 2 * d:].astype(BF16)
            wo = w_attn_out[a].astype(BF16)
            qscale = -attn_scale * LOG2E
            nb2 = -LOG2E * attn_bias[a]
            xb = yp.astype(BF16)
            qb = _matmul(xb, wq, BF16, scale=qscale)
            kf, kb = _kv_proj(xb, wk)
            vf, vb = _kv_proj(xb, wv)
            q_t = qb.reshape(seq, n_heads, HEAD_DIM).transpose(1, 2, 0)
            v_t = vb.reshape(seq // ATTN_TK, ATTN_TK, n_heads, HEAD_DIM).transpose(2, 0, 3, 1)
            mp = _attention_prompt(q_t, nb2, kb, v_t)
            kp_l.append(kf.reshape(batch, seq, n_heads, HEAD_DIM))
            vp_l.append(vf.reshape(batch, seq, n_heads, HEAD_DIM))
            qs = _matmul(ys, wq, F32)
            ksn, _ = _kv_proj(ys, wk)
            vsn, _ = _kv_proj(ys, wv)
            ms = _attention_decode(qs.reshape(dec_b, n_heads, HEAD_DIM), cache_k, cache_v, page_table,
                                   nb2.reshape(n_heads, 1), layer=a, qscale=qscale).reshape(dec_b, d)
            ks_l.append(ksn.reshape(dec_b, dec_t, n_heads, HEAD_DIM))
            vs_l.append(vsn.reshape(dec_b, dec_t, n_heads, HEAD_DIM))
        else:
            p = layer // 2
            wi = w_pool_in[p].astype(BF16)
            wg = w_pool_group[p].astype(BF16)
            wo = w_pool_out[p].astype(BF16)
            sc = pool_scale[p].reshape(1, d)
            up = _matmul(yp, wi, F32)
            mp = _pool_prompt(up, wg, sc)
            pp_l.append(up[seq - pool_buf:].reshape(batch, pool_buf, d))
            us = _matmul(ys, wi, F32)
            ms = _pool_sample(us, state_pool[p], wg, sc, pos1=past_len + 1)
            ps_l.append(jnp.concatenate([state_pool[p][:, 1:], us[:, None, :]], axis=1))
        yp = _proj_ln(mp, wo, yp, g0, b0, alpha=alpha)
        ys = _proj_ln(ms, wo, ys, g0, b0, alpha=alpha)
        ys, wu, wd = _mlp_cast(ys, w_mlp_up, w_mlp_down, layer, g1, b1, alpha=alpha)
        yp = _mlp(yp, wu, wd, g1, b1, alpha=alpha)

    return (yp.reshape(batch, seq, d), ys.reshape(dec_b, dec_t, d),
            jnp.stack(kp_l), jnp.stack(vp_l), jnp.stack(pp_l),
            jnp.stack(ks_l), jnp.stack(vs_l), jnp.stack(ps_l))
```

```python
import functools
import math

import jax
import jax.numpy as jnp
from jax import lax
from jax.experimental import pallas as pl
from jax.experimental.pallas import tpu as pltpu

F32 = jnp.float32
BF16 = jnp.bfloat16

HEAD_DIM = 128
POOL_WINDOWS = (2, 4, 8, 16)
LN_EPS = 1e-5

LANES = 128
SUBLANES = 8
VMEM_LIMIT_BYTES = 56 * 1024 * 1024

ATTN_TQ = 512
ATTN_SUB = 256
ATTN_TK = ATTN_TQ
ATTN_HEADS_PER_STEP = 2
ATTN_TOTAL_ROWS = 16
DEC_PAGES_PER_STEP = 4
LOG2E = math.log2(math.e)
MASKED_LOG2 = -1e30
ATTN_DEAD_LOG2 = -160.0


def _params(*semantics):
    return pltpu.CompilerParams(dimension_semantics=semantics, vmem_limit_bytes=VMEM_LIMIT_BYTES)


def _layer_norm(xf, g, b):
    mu = jnp.mean(xf, axis=-1, keepdims=True)
    xc = xf - mu
    var = jnp.mean(xc * xc, axis=-1, keepdims=True)
    return xc * lax.rsqrt(var + LN_EPS) * g + b


def _matmul_kernel(a_ref, b_ref, o_ref, *, scale):
    acc = jnp.dot(a_ref[...].astype(BF16), b_ref[...], preferred_element_type=F32)
    o_ref[...] = (acc if scale is None else acc * scale).astype(o_ref.dtype)


def _matmul(a, b, out_dtype, *, scale=None, tm=512):
    m, k = a.shape
    n = b.shape[1]
    tm = min(tm, m)
    return pl.pallas_call(
        functools.partial(_matmul_kernel, scale=scale),
        grid=(m // tm,),
        in_specs=[pl.BlockSpec((tm, k), lambda i: (i, 0)),
                  pl.BlockSpec((k, n), lambda i: (0, 0))],
        out_specs=pl.BlockSpec((tm, n), lambda i: (i, 0)),
        out_shape=jax.ShapeDtypeStruct((m, n), out_dtype),
        compiler_params=_params("arbitrary"),
        name="matmul",
    )(a, b)


def _store_head_rows(o_ref, h, val):
    rows, tile_heads, hd = o_ref.shape
    o_ref.reshape(rows * tile_heads, hd)[pl.ds(h, rows, stride=tile_heads), :] = val


def _kv_kernel(x_ref, w_ref, f_ref, b_ref):
    acc = jnp.dot(x_ref[...].astype(BF16), w_ref[...], preferred_element_type=F32)
    b_ref[...] = acc.astype(BF16)
    hd = f_ref.shape[-1]
    for h in range(f_ref.shape[1]):
        _store_head_rows(f_ref, h, acc[:, h * hd:(h + 1) * hd])


def _kv_proj(x, w, *, tm=512):
    m, d = x.shape
    n = w.shape[1]
    n_heads = n // HEAD_DIM
    tm = min(tm, m)
    return pl.pallas_call(
        _kv_kernel,
        grid=(m // tm,),
        in_specs=[pl.BlockSpec((tm, d), lambda i: (i, 0)),
                  pl.BlockSpec((d, n), lambda i: (0, 0))],
        out_specs=[pl.BlockSpec((tm, n_heads, HEAD_DIM), lambda i: (i, 0, 0)),
                   pl.BlockSpec((tm, n), lambda i: (i, 0))],
        out_shape=[jax.ShapeDtypeStruct((m, n_heads, HEAD_DIM), F32), jax.ShapeDtypeStruct((m, n), BF16)],
        compiler_params=_params("arbitrary"),
        name="kv_proj",
    )(x, w)


def _proj_ln_kernel(a_ref, w_ref, x_ref, g_ref, b_ref, o_ref, *, alpha):
    m = jnp.dot(a_ref[...].astype(BF16), w_ref[...], preferred_element_type=F32)
    o_ref[...] = _layer_norm(alpha * x_ref[...] + m, g_ref[...], b_ref[...])


def _proj_ln(a, w, x, g, b, *, alpha, tm=512):
    m, k = a.shape
    d = w.shape[1]
    tm = min(tm, m)
    return pl.pallas_call(
        functools.partial(_proj_ln_kernel, alpha=alpha),
        grid=(m // tm,),
        in_specs=[pl.BlockSpec((tm, k), lambda i: (i, 0)),
                  pl.BlockSpec((k, d), lambda i: (0, 0)),
                  pl.BlockSpec((tm, d), lambda i: (i, 0)),
                  pl.BlockSpec((1, d), lambda i: (0, 0)),
                  pl.BlockSpec((1, d), lambda i: (0, 0))],
        out_specs=pl.BlockSpec((tm, d), lambda i: (i, 0)),
        out_shape=jax.ShapeDtypeStruct((m, d), F32),
        compiler_params=_params("arbitrary"),
        name="proj_ln",
    )(a, w, x, g, b)


def _mlp_init(kf, x_ref, acc_ref, xb_ref):
    @pl.when(kf == 0)
    def _():
        xb_ref[...] = x_ref[...].astype(BF16)
        acc_ref[...] = jnp.zeros_like(acc_ref)


def _mlp_accumulate(xb_ref, wu, wd, acc_ref):
    h = jnp.dot(xb_ref[...], wu, preferred_element_type=F32)
    h = jnp.maximum(h, 0.0)
    acc_ref[...] += jnp.dot((h * h).astype(BF16), wd, preferred_element_type=F32)


def _mlp_finish(kf, n_kf, x_ref, g_ref, b_ref, o_ref, acc_ref, alpha):
    @pl.when(kf == n_kf - 1)
    def _():
        o_ref[...] = _layer_norm(alpha * x_ref[...] + acc_ref[...], g_ref[...], b_ref[...])


def _mlp_kernel(x_ref, wu_ref, wd_ref, g_ref, b_ref, o_ref, acc_ref, xb_ref, *, alpha):
    kf = pl.program_id(1)
    _mlp_init(kf, x_ref, acc_ref, xb_ref)
    _mlp_accumulate(xb_ref, wu_ref[...], wd_ref[...], acc_ref)
    _mlp_finish(kf, pl.num_programs(1), x_ref, g_ref, b_ref, o_ref, acc_ref, alpha)


def _mlp_cast_kernel(x_ref, wu_ref, wd_ref, g_ref, b_ref, o_ref, wub_ref, wdb_ref, acc_ref, xb_ref, *, alpha):
    kf = pl.program_id(0)
    _mlp_init(kf, x_ref, acc_ref, xb_ref)
    wu = wu_ref[...].astype(BF16)
    wd = wd_ref[...].astype(BF16)
    wub_ref[...] = wu
    wdb_ref[...] = wd
    _mlp_accumulate(xb_ref, wu, wd, acc_ref)
    _mlp_finish(kf, pl.num_programs(0), x_ref, g_ref, b_ref, o_ref, acc_ref, alpha)


def _mlp_cast(x, w_up, w_down, layer, g, b, *, alpha, tf=512):
    m, d = x.shape
    f = w_up.shape[2]
    return pl.pallas_call(
        functools.partial(_mlp_cast_kernel, alpha=alpha),
        grid=(f // tf,),
        in_specs=[pl.BlockSpec((m, d), lambda kf: (0, 0)),
                  pl.BlockSpec((None, d, tf), lambda kf: (layer, 0, kf)),
                  pl.BlockSpec((None, tf, d), lambda kf: (layer, kf, 0)),
                  pl.BlockSpec((1, d), lambda kf: (0, 0)),
                  pl.BlockSpec((1, d), lambda kf: (0, 0))],
        out_specs=[pl.BlockSpec((m, d), lambda kf: (0, 0)),
                   pl.BlockSpec((d, tf), lambda kf: (0, kf)),
                   pl.BlockSpec((tf, d), lambda kf: (kf, 0))],
        out_shape=[jax.ShapeDtypeStruct((m, d), F32), jax.ShapeDtypeStruct((d, f), BF16),
                   jax.ShapeDtypeStruct((f, d), BF16)],
        scratch_shapes=[pltpu.VMEM((m, d), F32), pltpu.VMEM((m, d), BF16)],
        compiler_params=_params("arbitrary"),
        name="mlp_cast",
    )(x, w_up, w_down, g, b)


def _mlp(x, wu, wd, g, b, *, alpha, tm=512, tf=1024):
    m, d = x.shape
    f = wu.shape[1]
    tm = min(tm, m)
    return pl.pallas_call(
        functools.partial(_mlp_kernel, alpha=alpha),
        grid=(m // tm, f // tf),
        in_specs=[pl.BlockSpec((tm, d), lambda i, kf: (i, 0)),
                  pl.BlockSpec((d, tf), lambda i, kf: (0, kf)),
                  pl.BlockSpec((tf, d), lambda i, kf: (kf, 0)),
                  pl.BlockSpec((1, d), lambda i, kf: (0, 0)),
                  pl.BlockSpec((1, d), lambda i, kf: (0, 0))],
        out_specs=pl.BlockSpec((tm, d), lambda i, kf: (i, 0)),
        out_shape=jax.ShapeDtypeStruct((m, d), F32),
        scratch_shapes=[pltpu.VMEM((tm, d), F32), pltpu.VMEM((tm, d), BF16)],
        compiler_params=_params("arbitrary", "arbitrary"),
        name="mlp",
    )(x, wu, wd, g, b)


def _log2_stay_and_beta(w):
    ls = jnp.minimum(w, 0.0) - jnp.log(1.0 + jnp.exp2(-jnp.abs(w))) * LOG2E
    return ls, ls - w


def _split_bf16(x):
    hi = x.astype(BF16)
    lo = (x - hi.astype(F32)).astype(BF16)
    return hi, lo


def _pad_left(x, width):
    missing = width - x.shape[1]
    return x if missing == 0 else jnp.concatenate([jnp.zeros((x.shape[0], missing), x.dtype), x], axis=1)


def _attn_block(k_ref, vt_ref, k_one, scan_mat, q_aug, j, state, diagonal):
    start = pl.multiple_of(j * ATTN_TK, ATTN_TK)
    heads = range(ATTN_HEADS_PER_STEP)
    subs = list(reversed(range(ATTN_TK // ATTN_SUB)))
    first_col = {sub: sub * ATTN_SUB if diagonal else 0 for sub in subs}
    scores = {}
    for hh in heads:
        for sub in subs:
            k_sub = k_ref[pl.ds(start + sub * ATTN_SUB, ATTN_SUB), hh * HEAD_DIM:(hh + 1) * HEAD_DIM]
            scores[hh, sub] = jnp.dot(jnp.concatenate([k_sub, k_one], axis=1), q_aug[hh][:, first_col[sub]:],
                                      preferred_element_type=F32)
    log_beta, log_stay = {}, {}
    for hh in heads:
        for sub in subs:
            ls, lb = _log2_stay_and_beta(scores[hh, sub])
            if diagonal:
                row = lax.broadcasted_iota(jnp.int32, ls.shape, 0)
                col = lax.broadcasted_iota(jnp.int32, ls.shape, 1)
                mask = row < col
                ls = jnp.where(mask, ls, 0.0)
                lb = jnp.where(mask, lb, MASKED_LOG2)
            log_beta[hh, sub] = lb
            log_stay[hh, sub] = ls.astype(BF16)
    scans = {key: jnp.dot(scan_mat, ls, preferred_element_type=F32) for key, ls in log_stay.items()}
    new_state = []
    for hh in heads:
        acc, carry = state[hh]
        weights = []
        for sub in subs:
            res = scans[hh, sub]
            a = jnp.exp2(log_beta[hh, sub] + res[:ATTN_SUB] + carry[:, first_col[sub]:])
            weights.append(_pad_left(a.astype(BF16), ATTN_TQ))
            carry = carry + _pad_left(res[ATTN_SUB:ATTN_SUB + 1], ATTN_TQ)
        a = jnp.concatenate(weights[::-1], axis=0)
        new_state.append((acc + jnp.dot(vt_ref[hh, j], a, preferred_element_type=F32), carry))
    return tuple(new_state)


def _attn_kernel(qt_ref, qbias_ref, k_ref, vt_ref, k_one_ref, scan_ref, o_ref):
    i = pl.program_id(1)
    scan_mat = scan_ref[...]
    k_one = k_one_ref[...]
    heads = range(ATTN_HEADS_PER_STEP)
    q_aug = [jnp.concatenate([qt_ref[hh], qbias_ref[hh]], axis=0) for hh in heads]

    def block(j, state, diagonal):
        return _attn_block(k_ref, vt_ref, k_one, scan_mat, q_aug, j, state, diagonal)

    def body(carried):
        t, _, state = carried
        live = jnp.max(jnp.concatenate([state[hh][1] for hh in heads], axis=0)) > ATTN_DEAD_LOG2
        return t + 1, live.astype(jnp.int32), block(i - 1 - t, state, False)

    zero = (jnp.zeros((HEAD_DIM, ATTN_TQ), F32), jnp.zeros((1, ATTN_TQ), F32))
    state = block(i, tuple(zero for _ in heads), True)
    _, _, state = lax.while_loop(lambda c: (c[0] < i) & (c[1] != 0), body, (jnp.int32(0), jnp.int32(1), state))
    for hh in heads:
        o_ref[:, hh * HEAD_DIM:(hh + 1) * HEAD_DIM] = state[hh][0].T.astype(o_ref.dtype)


def _attention_prompt(q_t, nbias2, k_b, v_t):
    n_heads, hd, s = q_t.shape
    hps = ATTN_HEADS_PER_STEP
    nb_hi = nbias2.astype(BF16)
    nb_lo = (nbias2 - nb_hi.astype(F32)).astype(BF16)
    q_bias = jnp.zeros((n_heads, hd, ATTN_TQ), BF16)
    q_bias = q_bias.at[:, 0, :].set(nb_hi[:, None]).at[:, 1, :].set(nb_lo[:, None])
    k_one = jnp.zeros((ATTN_SUB, hd), BF16).at[:, :2].set(1.0)
    tri = (jnp.arange(ATTN_SUB)[None, :] > jnp.arange(ATTN_SUB)[:, None]).astype(BF16)
    scan_mat = jnp.concatenate([tri, jnp.ones((ATTN_TOTAL_ROWS, ATTN_SUB), BF16)], axis=0)
    return pl.pallas_call(
        _attn_kernel,
        grid=(n_heads // hps, s // ATTN_TQ),
        in_specs=[pl.BlockSpec((hps, hd, ATTN_TQ), lambda h, i: (h, 0, i)),
                  pl.BlockSpec((hps, hd, ATTN_TQ), lambda h, i: (h, 0, 0)),
                  pl.BlockSpec((s, hps * hd), lambda h, i: (0, h)),
                  pl.BlockSpec((hps, s // ATTN_TK, hd, ATTN_TK), lambda h, i: (h, 0, 0, 0)),
                  pl.BlockSpec(k_one.shape, lambda h, i: (0, 0)),
                  pl.BlockSpec(scan_mat.shape, lambda h, i: (0, 0))],
        out_specs=pl.BlockSpec((ATTN_TQ, hps * hd), lambda h, i: (i, h)),
        out_shape=jax.ShapeDtypeStruct((s, n_heads * hd), BF16),
        compiler_params=_params("arbitrary", "arbitrary"),
        name="attn_prompt",
    )(q_t, q_bias, k_b, v_t, k_one, scan_mat)


def _head_rows(block_ref, h):
    page, tile_heads, hd = block_ref.shape
    return block_ref.reshape(page * tile_heads, hd)[pl.ds(h, page, stride=tile_heads), :]


def _decode_kernel(pt_ref, nb_ref, q_ref, *refs, qscale):
    npp = DEC_PAGES_PER_STEP
    n_heads, hd = q_ref.shape
    page, tile_heads, _ = refs[0].shape
    n_tiles = n_heads // tile_heads
    n_blocks = npp * n_tiles
    k_refs, v_refs = refs[:n_blocks], refs[n_blocks:2 * n_blocks]
    scan_ref, o_ref, q2_ref, acc_ref, carry_ref = refs[2 * n_blocks:]
    p = pl.program_id(1)
    n_pairs = n_heads // 2
    row = lax.broadcasted_iota(jnp.int32, (n_heads, 2 * hd), 0)
    col = lax.broadcasted_iota(jnp.int32, (n_heads, 2 * hd), 1)
    own_half = (row % 2) == (col // hd)
    row_tok = lax.broadcasted_iota(jnp.int32, (n_heads, 2 * page), 0)

    @pl.when(p == 0)
    def _():
        qs = q_ref[...] * qscale
        qq = jnp.concatenate([qs, qs], axis=1)
        for pr in range(n_pairs):
            q2_ref[pr] = jnp.where((row // 2 == pr) & own_half, qq, 0.0).astype(BF16)
        acc_ref[...] = jnp.zeros_like(acc_ref)
        carry_ref[...] = jnp.zeros_like(carry_ref)

    def head_rows(page_refs, pg, h):
        return _head_rows(page_refs[pg * n_tiles + h // tile_heads], h % tile_heads)

    def pair_tile(page_refs, older, newer, pr):
        rows = [jnp.concatenate([head_rows(page_refs, pg, 2 * pr), head_rows(page_refs, pg, 2 * pr + 1)], axis=1)
                for pg in (older, newer)]
        return jnp.concatenate(rows, axis=0).astype(BF16)

    nb = nb_ref[...]
    acc = acc_ref[...]
    carry = carry_ref[...]
    page_pairs = [(c + 1, c) for c in range(0, npp, 2)]
    scores = []
    for older, newer in page_pairs:
        z = None
        for pr in range(n_pairs):
            zp = lax.dot_general(q2_ref[pr], pair_tile(k_refs, older, newer, pr), (((1,), (1,)), ((), ())),
                                 preferred_element_type=F32)
            z = zp if z is None else z + zp
        scores.append(z)
    log_beta, scans = [], []
    for z in scores:
        ls, lb = _log2_stay_and_beta(z + nb)
        hi, lo = _split_bf16(ls)
        log_beta.append(lb)
        scans.append(jnp.dot(jnp.concatenate([hi, lo], axis=1), scan_ref[...], preferred_element_type=F32))
    weights = []
    for lb, scan in zip(log_beta, scans):
        weights.append(jnp.exp2(lb + scan[:, :2 * page] + jnp.concatenate([carry] * (2 * page // LANES), axis=1)))
        carry = carry + scan[:, 2 * page:]
    for (older, newer), a in zip(page_pairs, weights):
        for pr in range(n_pairs):
            a_pr = jnp.where(row_tok // 2 == pr, a, 0.0).astype(BF16)
            acc = acc + jnp.dot(a_pr, pair_tile(v_refs, older, newer, pr), preferred_element_type=F32)
    acc_ref[...] = acc
    carry_ref[...] = carry

    @pl.when(p == pl.num_programs(1) - 1)
    def _():
        o_ref[...] = jnp.where(row[:, :hd] % 2 == 0, acc[:, :hd], acc[:, hd:]).astype(o_ref.dtype)


def _attention_decode(q, cache_k, cache_v, page_table, nbias2, *, layer, qscale):
    bsz, n_heads, hd = q.shape
    page = cache_k.shape[2]
    n_pages = page_table.shape[1]
    npp = DEC_PAGES_PER_STEP
    assert npp % 2 == 0 and n_pages % npp == 0 and n_heads % 2 == 0
    tri = (jnp.arange(2 * page)[:, None] > jnp.arange(2 * page)[None, :]).astype(BF16)
    scan_mat = jnp.tile(jnp.concatenate([tri, jnp.ones((2 * page, LANES), BF16)], axis=1), (2, 1))

    tile_heads = SUBLANES if n_heads % SUBLANES == 0 else n_heads
    n_tiles = n_heads // tile_heads

    def page_spec(c, t):
        return pl.BlockSpec((None, None, page, tile_heads, hd),
                            lambda b, p, pt: (layer, pt[b, n_pages - 1 - (p * npp + c)], 0, t, 0))

    page_specs = [page_spec(c, t) for c in range(npp) for t in range(n_tiles)]
    grid_spec = pltpu.PrefetchScalarGridSpec(
        num_scalar_prefetch=1,
        grid=(bsz, n_pages // npp),
        in_specs=[pl.BlockSpec((n_heads, 1), lambda b, p, pt: (0, 0)),
                  pl.BlockSpec((None, n_heads, hd), lambda b, p, pt: (b, 0, 0))]
                 + page_specs + page_specs
                 + [pl.BlockSpec(scan_mat.shape, lambda b, p, pt: (0, 0))],
        out_specs=pl.BlockSpec((None, n_heads, hd), lambda b, p, pt: (b, 0, 0)),
        scratch_shapes=[pltpu.VMEM((n_heads // 2, n_heads, 2 * hd), BF16),
                        pltpu.VMEM((n_heads, 2 * hd), F32),
                        pltpu.VMEM((n_heads, LANES), F32)],
    )
    return pl.pallas_call(
        functools.partial(_decode_kernel, qscale=qscale),
        grid_spec=grid_spec,
        out_shape=jax.ShapeDtypeStruct((bsz, n_heads, hd), BF16),
        compiler_params=_params("arbitrary", "arbitrary"),
        name="attn_decode",
    )(page_table, nbias2, q, *([cache_k] * len(page_specs)), *([cache_v] * len(page_specs)), scan_mat)


def _group_map(pooled, wg_ref, sc_ref, o_ref):
    gd = wg_ref.shape[1]
    for g in range(len(POOL_WINDOWS)):
        lanes = slice(g * gd, (g + 1) * gd)
        h = jnp.dot(pooled[g].astype(BF16), wg_ref[g], preferred_element_type=F32)
        o_ref[:, lanes] = (h * sc_ref[:, lanes]).astype(o_ref.dtype)


def _pool_prompt_kernel(u_ref, prev_ref, wg_ref, sc_ref, o_ref, *, halo):
    i = pl.program_id(0)
    tm = u_ref.shape[0]
    gd = wg_ref.shape[1]
    prev = jnp.where(i > 0, prev_ref[...], 0.0)
    pos1 = i * tm + lax.broadcasted_iota(jnp.int32, (tm, 1), 0) + 1
    pooled = []
    for g, win in enumerate(POOL_WINDOWS):
        lanes = slice(g * gd, (g + 1) * gd)
        u = u_ref[:, lanes]
        ext = jnp.concatenate([prev[:, lanes], u], axis=0)
        span = 1
        while span < win:
            ext = ext + pltpu.roll(ext, span, 0)
            span *= 2
        cnt = jnp.minimum(pos1, win).astype(F32)
        pooled.append(ext[halo:] / cnt - u)
    _group_map(pooled, wg_ref, sc_ref, o_ref)


def _pool_prompt(u, wg, sc, *, tm=256):
    s, d = u.shape
    halo = 16
    gd = wg.shape[1]
    return pl.pallas_call(
        functools.partial(_pool_prompt_kernel, halo=halo),
        grid=(s // tm,),
        in_specs=[pl.BlockSpec((tm, d), lambda i: (i, 0)),
                  pl.BlockSpec((halo, d), lambda i: (jnp.maximum(i * (tm // halo) - 1, 0), 0)),
                  pl.BlockSpec((len(POOL_WINDOWS), gd, gd), lambda i: (0, 0, 0)),
                  pl.BlockSpec((1, d), lambda i: (0, 0))],
        out_specs=pl.BlockSpec((tm, d), lambda i: (i, 0)),
        out_shape=jax.ShapeDtypeStruct((s, d), BF16),
        compiler_params=_params("arbitrary"),
        name="pool_prompt",
    )(u, u, wg, sc)


def _pool_sample_kernel(u_ref, buf_ref, wg_ref, sc_ref, o_ref, *, pos1):
    gd = wg_ref.shape[1]
    d = u_ref.shape[1]
    nbuf = buf_ref.shape[1] // d
    pooled = []
    for g, win in enumerate(POOL_WINDOWS):
        lanes = slice(g * gd, (g + 1) * gd)
        u = u_ref[:, lanes]
        total = u
        for r in range(nbuf - (win - 1), nbuf):
            total = total + buf_ref[:, r * d + g * gd:r * d + (g + 1) * gd]
        pooled.append(total / float(min(pos1, win)) - u)
    _group_map(pooled, wg_ref, sc_ref, o_ref)


def _pool_sample(u, buf, wg, sc, *, pos1, tb=32):
    bsz, d = u.shape
    nbuf = buf.shape[1]
    gd = wg.shape[1]
    buf = buf.reshape(bsz, nbuf * d)
    return pl.pallas_call(
        functools.partial(_pool_sample_kernel, pos1=pos1),
        grid=(bsz // tb,),
        in_specs=[pl.BlockSpec((tb, d), lambda i: (i, 0)),
                  pl.BlockSpec((tb, nbuf * d), lambda i: (i, 0)),
                  pl.BlockSpec((len(POOL_WINDOWS), gd, gd), lambda i: (0, 0, 0)),
                  pl.BlockSpec((1, d), lambda i: (0, 0))],
        out_specs=pl.BlockSpec((tb, d), lambda i: (i, 0)),
        out_shape=jax.ShapeDtypeStruct((bsz, d), BF16),
        compiler_params=_params("arbitrary"),
        name="pool_sample",
    )(u, buf, wg, sc)


def kernel(x_prompt, x_sample, cache_k, cache_v, state_pool, page_table, ln_g, ln_b, w_qkv, attn_bias,
           w_attn_out, w_pool_in, w_pool_group, pool_scale, w_pool_out, w_mlp_up, w_mlp_down):
    batch, seq, d = x_prompt.shape
    dec_b, dec_t, _ = x_sample.shape
    depth = ln_g.shape[0]
    n_heads = d // HEAD_DIM
    page = cache_k.shape[2]
    past_len = page_table.shape[1] * page
    pool_buf = state_pool.shape[2]
    assert batch == 1 and dec_t == 1 and pool_buf == max(POOL_WINDOWS) - 1
    alpha = (2.0 * depth) ** 0.25
    attn_scale = HEAD_DIM ** -0.5

    yp = x_prompt.reshape(seq, d)
    ys = x_sample.reshape(dec_b, d)

    kp_l, vp_l, ks_l, vs_l, pp_l, ps_l = [], [], [], [], [], []
    for layer in range(depth):
        g0, b0 = ln_g[layer, 0].reshape(1, d), ln_b[layer, 0].reshape(1, d)
        g1, b1 = ln_g[layer, 1].reshape(1, d), ln_b[layer, 1].reshape(1, d)
        if layer % 2 == 0:
            a = layer // 2
            wq = w_qkv[a, :, :d].astype(BF16)
            wk = w_qkv[a, :, d:2 * d].astype(BF16)
            wv = w_qkv[a, :, 2 * d:].astype(BF16)
            wo = w_attn_out[a].astype(BF16)
            qscale = -attn_scale * LOG2E
            nb2 = -LOG2E * attn_bias[a]
            qb = _matmul(yp, wq, BF16, scale=qscale)
            kf, kb = _kv_proj(yp, wk)
            vf, vb = _kv_proj(yp, wv)
            q_t = qb.reshape(seq, n_heads, HEAD_DIM).transpose(1, 2, 0)
            v_t = vb.reshape(seq // ATTN_TK, ATTN_TK, n_heads, HEAD_DIM).transpose(2, 0, 3, 1)
            mp = _attention_prompt(q_t, nb2, kb, v_t)
            kp_l.append(kf.reshape(batch, seq, n_heads, HEAD_DIM))
            vp_l.append(vf.reshape(batch, seq, n_heads, HEAD_DIM))
            qs = _matmul(ys, wq, F32)
            ksn, _ = _kv_proj(ys, wk)
            vsn, _ = _kv_proj(ys, wv)
            ms = _attention_decode(qs.reshape(dec_b, n_heads, HEAD_DIM), cache_k, cache_v, page_table,
                                   nb2.reshape(n_heads, 1), layer=a, qscale=qscale).reshape(dec_b, d)
            ks_l.append(ksn.reshape(dec_b, dec_t, n_heads, HEAD_DIM))
            vs_l.append(vsn.reshape(dec_b, dec_t, n_heads, HEAD_DIM))
        else:
            p = layer // 2
            wi = w_pool_in[p].astype(BF16)
            wg = w_pool_group[p].astype(BF16)
            wo = w_pool_out[p].astype(BF16)
            sc = pool_scale[p].reshape(1, d)
            up = _matmul(yp, wi, F32)
            mp = _pool_prompt(up, wg, sc)
            pp_l.append(up[seq - pool_buf:].reshape(batch, pool_buf, d))
            us = _matmul(ys, wi, F32)
            ms = _pool_sample(us, state_pool[p], wg, sc, pos1=past_len + 1)
            ps_l.append(jnp.concatenate([state_pool[p][:, 1:], us[:, None, :]], axis=1))
        yp = _proj_ln(mp, wo, yp, g0, b0, alpha=alpha)
        ys = _proj_ln(ms, wo, ys, g0, b0, alpha=alpha)
        ys, wu, wd = _mlp_cast(ys, w_mlp_up, w_mlp_down, layer, g1, b1, alpha=alpha)
        yp = _mlp(yp, wu, wd, g1, b1, alpha=alpha)

    return (yp.reshape(batch, seq, d), ys.reshape(dec_b, dec_t, d),
            jnp.stack(kp_l), jnp.stack(vp_l), jnp.stack(pp_l),
            jnp.stack(ks_l), jnp.stack(vs_l), jnp.stack(ps_l))
```
